```python
import jax, jax.numpy as jnp
from jax import lax
import numpy as np

D_MODEL = 1024
BATCH = 8
SEQ = 2048
DEPTH = 1

C_CONV = 512
CONV_WIDTH = 31
N_HEADS = 8
HEAD_DIM = 64
H_IDX = 8
D_IDX = 64
TOPK_MAX = 256
Q_BLOCK = 128
D_FF = 2816
N_BRANCH = 2
RMS_EPS = 1e-6
LN_EPS = 1e-5

IN_SIZES = (2 * C_CONV, N_HEADS * HEAD_DIM, HEAD_DIM, HEAD_DIM, H_IDX * D_IDX, D_IDX, H_IDX, N_BRANCH * D_MODEL)
N_IN = 2 * C_CONV + N_HEADS * HEAD_DIM + 2 * HEAD_DIM + H_IDX * D_IDX + D_IDX + H_IDX + N_BRANCH * D_MODEL

kernel_name = "hybrid_conformer_conv_dsa_gated_block"


def _rmsnorm(x, g):
    x32 = x.astype(jnp.float32)
    y = x32 * lax.rsqrt(jnp.mean(x32 * x32, axis=-1, keepdims=True) + RMS_EPS)
    return (y * g.astype(jnp.float32)).astype(x.dtype)


def _layernorm(x, g, b):
    x32 = x.astype(jnp.float32)
    mu = jnp.mean(x32, axis=-1, keepdims=True)
    xc = x32 - mu
    y = xc * lax.rsqrt(jnp.mean(xc * xc, axis=-1, keepdims=True) + LN_EPS)
    return (y * g.astype(jnp.float32) + b.astype(jnp.float32)).astype(x.dtype)


def _alibi_slopes(n):
    return jnp.exp2(-8.0 * jnp.arange(1, n + 1, dtype=jnp.float32) / n)


def _split_cols(u):
    parts, off = [], 0
    for sz in IN_SIZES:
        parts.append(u[..., off:off + sz])
        off += sz
    return parts


def _conv_module(u, conv_w, conv_b, ln_g, ln_b, w_pw_out):
    a, gt = u[..., :C_CONV], u[..., C_CONV:]
    h = a * jax.nn.sigmoid(gt)
    h = lax.conv_general_dilated(
        h, conv_w[:, None, :].astype(h.dtype), window_strides=(1,),
        padding=[(CONV_WIDTH - 1, 0)],
        dimension_numbers=('NWC', 'WIO', 'NWC'), feature_group_count=C_CONV) + conv_b
    h = jax.nn.silu(_layernorm(h, ln_g, ln_b))
    return h @ w_pw_out


def _dsa_attention(q, k, v, q_idx, k_idx, w_idx, slopes):
    B, L = q.shape[0], q.shape[1]
    topk = min(TOPK_MAX, L // 4)
    nb = L // Q_BLOCK

    def to_blocks(a):
        return jnp.moveaxis(a.reshape((B, nb, Q_BLOCK) + a.shape[2:]), 1, 0)

    s_pos = jnp.arange(L)
    gather = jax.vmap(lambda table, ii: table[ii])

    def block(xs):
        start, qb, qib, wb = xs
        t = start + jnp.arange(Q_BLOCK)
        causal = s_pos[None, :] <= t[:, None]
        rel = jax.nn.relu(jnp.einsum('bqhd,bsd->bqhs', qib, k_idx))
        score = jnp.einsum('bqh,bqhs->bqs', wb, rel).astype(jnp.float32)
        score = jnp.where(causal[None], score, -jnp.inf)
        _, idx = lax.top_k(score, topk)
        k_sel = gather(k, idx)
        v_sel = gather(v, idx)
        att = jnp.einsum('bqhd,bqkd->bhqk', qb, k_sel).astype(jnp.float32) * (HEAD_DIM ** -0.5)
        dist = (t[None, :, None] - idx).astype(jnp.float32)
        att = att - slopes[None, :, None, None] * dist[:, None]
        valid = (idx <= t[None, :, None])[:, None]
        att = jnp.where(valid, att, -jnp.inf)
        p = jax.nn.softmax(att, axis=-1).astype(v.dtype)
        return jnp.einsum('bhqk,bqkd->bqhd', p, v_sel)

    starts = jnp.arange(nb, dtype=jnp.int32) * Q_BLOCK
    out = lax.map(block, (starts, to_blocks(q), to_blocks(q_idx), to_blocks(w_idx)))
    return jnp.moveaxis(out, 0, 1).reshape(B, L, N_HEADS * HEAD_DIM)


def setup_inputs(seed: int = 0) -> dict:
    key = jax.random.key(seed)
    ks = jax.random.split(key, 20)
    f32 = jnp.float32

    def nrm(k, shape, fan_in):
        return jax.random.normal(k, shape, f32) * (fan_in ** -0.5)

    def gain(k, shape):
        return 1.0 + 0.02 * jax.random.normal(k, shape, f32)

    return {
        "x": jax.random.normal(ks[0], (BATCH, SEQ, D_MODEL), f32),
        "norm_mix": gain(ks[1], (DEPTH, D_MODEL)),
        "w_in": nrm(ks[2], (DEPTH, D_MODEL, N_IN), D_MODEL),
        "b_gate": 0.02 * jax.random.normal(ks[3], (DEPTH, N_BRANCH * D_MODEL), f32),
        "conv_w": nrm(ks[4], (DEPTH, CONV_WIDTH, C_CONV), CONV_WIDTH),
        "conv_b": 0.02 * jax.random.normal(ks[5], (DEPTH, C_CONV), f32),
        "conv_ln_g": gain(ks[6], (DEPTH, C_CONV)),
        "conv_ln_b": 0.02 * jax.random.normal(ks[7], (DEPTH, C_CONV), f32),
        "w_conv_out": nrm(ks[8], (DEPTH, C_CONV, D_MODEL), C_CONV),
        "w_attn_out": nrm(ks[9], (DEPTH, N_HEADS * HEAD_DIM, D_MODEL), N_HEADS * HEAD_DIM),
        "w_out": nrm(ks[10], (DEPTH, D_MODEL, D_MODEL), D_MODEL),
        "norm_ffn": gain(ks[11], (DEPTH, D_MODEL)),
        "w_ffn_gate": nrm(ks[12], (DEPTH, D_MODEL, D_FF), D_MODEL),
        "w_ffn_up": nrm(ks[13], (DEPTH, D_MODEL, D_FF), D_MODEL),
        "w_ffn_down": nrm(ks[14], (DEPTH, D_FF, D_MODEL), D_FF),
        "norm_final": gain(ks[15], (D_MODEL,)),
    }


def reference(x, norm_mix, w_in, b_gate, conv_w, conv_b, conv_ln_g, conv_ln_b,
              w_conv_out, w_attn_out, w_out, norm_ffn, w_ffn_gate, w_ffn_up,
              w_ffn_down, norm_final):
    B, L, _ = x.shape
    slopes = _alibi_slopes(N_HEADS)
    for layer in range(DEPTH):
        h = _rmsnorm(x, norm_mix[layer])
        u = h @ w_in[layer]
        u_conv, u_q, u_k, u_v, u_qi, u_ki, u_wi, u_g = _split_cols(u)
        y_conv = _conv_module(u_conv, conv_w[layer], conv_b[layer], conv_ln_g[layer],
                              conv_ln_b[layer], w_conv_out[layer])
        q = u_q.reshape(B, L, N_HEADS, HEAD_DIM)
        q_idx = u_qi.reshape(B, L, H_IDX, D_IDX) * (D_IDX ** -0.5)
        w_idx = u_wi * (H_IDX ** -0.5)
        y_attn = _dsa_attention(q, u_k, u_v, q_idx, u_ki, w_idx, slopes) @ w_attn_out[layer]
        g = jax.nn.sigmoid(u_g + b_gate[layer])
        merged = g[..., :D_MODEL] * y_conv + g[..., D_MODEL:] * y_attn
        x = x + merged @ w_out[layer]
        hf = _rmsnorm(x, norm_ffn[layer])
        x = x + (jax.nn.silu(hf @ w_ffn_gate[layer]) * (hf @ w_ffn_up[layer])) @ w_ffn_down[layer]
    return _rmsnorm(x, norm_final)
```

```python
import functools

import jax
import jax.numpy as jnp
from jax import lax
from jax.experimental import pallas as pl
from jax.experimental.pallas import tpu as pltpu

C_CONV = 512
CONV_WIDTH = 31
N_HEADS = 8
HEAD_DIM = 64
H_IDX = 8
D_IDX = 64
TOPK_MAX = 256
RMS_EPS = 1e-6
LN_EPS = 1e-5

V7X_VMEM_BYTES = 64 * 1024 * 1024
VMEM_LIMIT = 56 * 1024 * 1024

TM_IN = 512
TM_POST = 256
TQ = 256
CK = 256
CONV_ROWS = 256
CONV_PAD = 32
NEG_BIG = -1e30
INT_MIN = -2 ** 31

_NT = (((1,), (1,)), ((), ()))


def _dot(a, b):
    return jnp.dot(a, b, preferred_element_type=jnp.float32)


def _sigmoid(x):
    return 1.0 / (1.0 + jnp.exp(-x))


def _inproj_kernel(x_ref, gn_ref, wc_ref, wg_ref, bg_ref, wkk_ref, wt_ref,
                   hglu_ref, gate_ref, k_ref, ki_ref, qt_ref, qit_ref, vt_ref, wit_ref):
    x = x_ref[0]
    ms = jnp.mean(x * x, axis=-1, keepdims=True)
    h = (x * lax.rsqrt(ms + RMS_EPS) * gn_ref[...]).astype(jnp.bfloat16)

    uc = _dot(h, wc_ref[...])
    hglu_ref[0] = uc[:, :C_CONV] * _sigmoid(uc[:, C_CONV:])

    ug = _dot(h, wg_ref[...]) + bg_ref[...]
    gate_ref[0] = _sigmoid(ug).astype(jnp.bfloat16)

    ukk = _dot(h, wkk_ref[...])
    k_ref[0] = ukk[:, :HEAD_DIM].astype(jnp.bfloat16)
    ki_ref[0] = ukk[:, HEAD_DIM:].astype(jnp.bfloat16)

    ut = lax.dot_general(wt_ref[...], h, _NT, preferred_element_type=jnp.float32)
    nq = N_HEADS * HEAD_DIM
    nqi = H_IDX * D_IDX
    qt_ref[0] = ut[:nq].astype(jnp.bfloat16)
    qit_ref[0] = ut[nq:nq + nqi].astype(jnp.bfloat16)
    vt = ut[nq + nqi:nq + nqi + HEAD_DIM].astype(jnp.bfloat16)
    for c in range(TM_IN // CK):
        vt_ref[0, c] = vt[:, c * CK:(c + 1) * CK]
    o = nq + nqi + HEAD_DIM
    wit_ref[0] = ut[o:o + H_IDX] * (H_IDX ** -0.5)


def _inproj(x, gn, wc, wg, bg, wkk, wt):
    B, L, D = x.shape
    nt = L // TM_IN
    const = lambda b, i: (0, 0)
    row = lambda b, i: (b, i, 0)
    col = lambda b, i: (b, 0, i)
    bf = jnp.bfloat16
    out_shape = (
        jax.ShapeDtypeStruct((B, L, C_CONV), jnp.float32),
        jax.ShapeDtypeStruct((B, L, 2 * D), bf),
        jax.ShapeDtypeStruct((B, L, HEAD_DIM), bf),
        jax.ShapeDtypeStruct((B, L, D_IDX), bf),
        jax.ShapeDtypeStruct((B, N_HEADS * HEAD_DIM, L), bf),
        jax.ShapeDtypeStruct((B, H_IDX * D_IDX, L), bf),
        jax.ShapeDtypeStruct((B, L // CK, HEAD_DIM, CK), bf),
        jax.ShapeDtypeStruct((B, H_IDX, L), jnp.float32),
    )
    out_specs = (
        pl.BlockSpec((1, TM_IN, C_CONV), row),
        pl.BlockSpec((1, TM_IN, 2 * D), row),
        pl.BlockSpec((1, TM_IN, HEAD_DIM), row),
        pl.BlockSpec((1, TM_IN, D_IDX), row),
        pl.BlockSpec((1, N_HEADS * HEAD_DIM, TM_IN), col),
        pl.BlockSpec((1, H_IDX * D_IDX, TM_IN), col),
        pl.BlockSpec((1, TM_IN // CK, HEAD_DIM, CK), lambda b, i: (b, i, 0, 0)),
        pl.BlockSpec((1, H_IDX, TM_IN), col),
    )
    in_specs = [
        pl.BlockSpec((1, TM_IN, D), row),
        pl.BlockSpec(gn.shape, const),
        pl.BlockSpec(wc.shape, const),
        pl.BlockSpec(wg.shape, const),
        pl.BlockSpec(bg.shape, const),
        pl.BlockSpec(wkk.shape, const),
        pl.BlockSpec(wt.shape, const),
    ]
    return pl.pallas_call(
        _inproj_kernel,
        grid=(B, nt),
        in_specs=in_specs,
        out_specs=out_specs,
        out_shape=out_shape,
        compiler_params=pltpu.CompilerParams(
            dimension_semantics=("parallel", "parallel"), vmem_limit_bytes=VMEM_LIMIT),
        name="inproj",
    )(x, gn, wc, wg, bg, wkk, wt)


def _conv_kernel(h_ref, w_ref, b_ref, lg_ref, lb_ref, o_ref, pad_ref):
    L = h_ref.shape[1]
    pad_ref[0:CONV_PAD, :] = jnp.zeros((CONV_PAD, C_CONV), jnp.float32)
    pad_ref[CONV_PAD:, :] = h_ref[0]
    shift = CONV_PAD - (CONV_WIDTH - 1)

    def tile(t, carry):
        r0 = pl.multiple_of(t * CONV_ROWS, CONV_ROWS)
        acc = jnp.zeros((CONV_ROWS, C_CONV), jnp.float32) + b_ref[...]
        win = pad_ref[pl.ds(r0, CONV_ROWS + CONV_PAD), :]
        for r in range(8):
            rolled = win if r == 0 else pltpu.roll(win, CONV_ROWS + CONV_PAD - r, axis=0)
            for k in range(CONV_WIDTH):
                if (shift + k) % 8 == r:
                    off = shift + k - r
                    acc = acc + w_ref[k:k + 1, :] * rolled[off:off + CONV_ROWS]
        mu = jnp.mean(acc, axis=-1, keepdims=True)
        xc = acc - mu
        var = jnp.mean(xc * xc, axis=-1, keepdims=True)
        y = xc * lax.rsqrt(var + LN_EPS) * lg_ref[...] + lb_ref[...]
        o_ref[0, pl.ds(r0, CONV_ROWS), :] = (y * _sigmoid(y)).astype(jnp.bfloat16)
        return carry

    lax.fori_loop(0, L // CONV_ROWS, tile, 0)


def _conv_branch(hglu, conv_w, conv_b, ln_g, ln_b):
    B, L, _ = hglu.shape
    const = lambda b: (0, 0)
    return pl.pallas_call(
        _conv_kernel,
        grid=(B,),
        in_specs=[
            pl.BlockSpec((1, L, C_CONV), lambda b: (b, 0, 0)),
            pl.BlockSpec(conv_w.shape, const),
            pl.BlockSpec(conv_b.shape, const),
            pl.BlockSpec(ln_g.shape, const),
            pl.BlockSpec(ln_b.shape, const),
        ],
        out_specs=pl.BlockSpec((1, L, C_CONV), lambda b: (b, 0, 0)),
        out_shape=jax.ShapeDtypeStruct((B, L, C_CONV), jnp.bfloat16),
        scratch_shapes=[pltpu.VMEM((CONV_PAD + L, C_CONV), jnp.float32)],
        compiler_params=pltpu.CompilerParams(
            dimension_semantics=("parallel",), vmem_limit_bytes=VMEM_LIMIT),
        name="conv_branch",
    )(hglu, conv_w, conv_b, ln_g, ln_b)


def _dsa_kernel(qit_ref, wit_ref, ki_ref, qt_ref, k_ref, vt_ref, o_ref,
                score_ref, bias_ref, ot_ref, *, topk):
    j = pl.program_id(1)
    nck = j + 1
    t_q = j * TQ + lax.broadcasted_iota(jnp.int32, (CK, TQ), 1)
    row_iota = lax.broadcasted_iota(jnp.int32, (CK, TQ), 0)

    def score_chunk(c, carry):
        kic = ki_ref[0, pl.ds(pl.multiple_of(c * CK, CK), CK), :]
        score = jnp.zeros((CK, TQ), jnp.float32)
        for h in range(H_IDX):
            rel = _dot(kic, qit_ref[0, h * D_IDX:(h + 1) * D_IDX, :])
            score = score + wit_ref[0, h:h + 1, :] * jnp.maximum(rel, 0.0)
        causal = (c * CK + row_iota) <= t_q
        score_ref[c] = jnp.where(causal, score, -jnp.inf)
        return carry

    lax.fori_loop(0, nck, score_chunk, 0)

    def count(pred_fn):
        def body(c, acc):
            m = pred_fn(score_ref[c]).astype(jnp.int32)
            return acc + jnp.sum(m.reshape(CK // 8, 8, TQ), axis=0)
        acc = lax.fori_loop(0, nck, body, jnp.zeros((8, TQ), jnp.int32))
        return jnp.sum(acc, axis=0, keepdims=True)

    def key_to_float(key):
        return pltpu.bitcast(jnp.where(key < 0, key ^ 0x7FFFFFFF, key), jnp.float32)

    def bit_step(i, u):
        cand_u = u | jnp.left_shift(jnp.int32(1), 31 - i)
        cand = key_to_float(cand_u ^ INT_MIN)
        cnt = count(lambda sc: sc >= cand)
        return jnp.where(cnt >= topk, cand_u, u)

    u = lax.fori_loop(0, 32, bit_step, jnp.zeros((1, TQ), jnp.int32))
    few = u == 0
    thr = jnp.where(few, -jnp.inf, key_to_float(u ^ INT_MIN))
    cnt_gt = count(lambda sc: sc > thr)
    need = jnp.where(few, 0, topk - cnt_gt).astype(jnp.float32)

    tri = (lax.broadcasted_iota(jnp.int32, (CK, CK), 1)
           <= lax.broadcasted_iota(jnp.int32, (CK, CK), 0)).astype(jnp.bfloat16)

    def select_chunk(c, carry):
        sc = score_ref[c]
        eq = sc == thr
        rank = _dot(tri, jnp.where(eq, 1.0, 0.0).astype(jnp.bfloat16)) + carry
        eq_rank = jnp.where(eq, rank, 3e38)
        bias = jnp.where(sc > thr, 0.0, jnp.where(eq_rank <= need, 0.0, NEG_BIG))
        bias_ref[c] = bias
        return rank[CK - 1:CK, :]

    lax.fori_loop(0, nck, select_chunk, jnp.zeros((1, TQ), jnp.float32))

    for h in range(N_HEADS):
        slope = 2.0 ** (-8.0 * (h + 1) / N_HEADS)
        qth = qt_ref[0, h * HEAD_DIM:(h + 1) * HEAD_DIM, :]

        def att_chunk(c, carry, qth=qth, slope=slope):
            m, l, acc = carry
            kc = k_ref[0, pl.ds(pl.multiple_of(c * CK, CK), CK), :]
            pos = (c * CK + row_iota).astype(jnp.float32)
            s = _dot(kc, qth) + slope * pos + bias_ref[c]
            m_new = jnp.maximum(m, jnp.max(s, axis=0, keepdims=True))
            alpha = jnp.exp(m - m_new)
            p = jnp.exp(s - m_new)
            l = alpha * l + jnp.sum(p, axis=0, keepdims=True)
            acc = alpha * acc + _dot(vt_ref[0, c], p.astype(jnp.bfloat16))
            return m_new, l, acc

        m0 = jnp.full((1, TQ), NEG_BIG, jnp.float32)
        l0 = jnp.zeros((1, TQ), jnp.float32)
        a0 = jnp.zeros((HEAD_DIM, TQ), jnp.float32)
        _, l, acc = lax.fori_loop(0, nck, att_chunk, (m0, l0, a0))
        ot_ref[h * HEAD_DIM:(h + 1) * HEAD_DIM, :] = acc / l

    o_ref[0] = ot_ref[...].T.astype(jnp.bfloat16)


def _dsa(qit, wit, ki, qt, k, vt):
    B, _, L = qt.shape
    nq = L // TQ
    nck = L // CK
    topk = min(TOPK_MAX, L // 4)
    per_b = lambda b, j: (b, 0, 0)
    col = lambda b, j: (b, 0, j)
    return pl.pallas_call(
        functools.partial(_dsa_kernel, topk=topk),
        grid=(B, nq),
        in_specs=[
            pl.BlockSpec((1, H_IDX * D_IDX, TQ), col),
            pl.BlockSpec((1, H_IDX, TQ), col),
            pl.BlockSpec((1, L, D_IDX), per_b),
            pl.BlockSpec((1, N_HEADS * HEAD_DIM, TQ), col),
            pl.BlockSpec((1, L, HEAD_DIM), per_b),
            pl.BlockSpec((1, nck, HEAD_DIM, CK), lambda b, j: (b, 0, 0, 0)),
        ],
        out_specs=pl.BlockSpec((1, TQ, N_HEADS * HEAD_DIM), lambda b, j: (b, j, 0)),
        out_shape=jax.ShapeDtypeStruct((B, L, N_HEADS * HEAD_DIM), jnp.bfloat16),
        scratch_shapes=[
            pltpu.VMEM((nck, CK, TQ), jnp.float32),
            pltpu.VMEM((nck, CK, TQ), jnp.float32),
            pltpu.VMEM((N_HEADS * HEAD_DIM, TQ), jnp.float32),
        ],
        compiler_params=pltpu.CompilerParams(
            dimension_semantics=("parallel", "parallel"), vmem_limit_bytes=VMEM_LIMIT),
        name="dsa_attention",
    )(qit, wit, ki, qt, k, vt)


def _post_kernel(x_ref, hc_ref, at_ref, g_ref, wco_ref, wao_ref, wo_ref, nf_ref,
                 wfg_ref, wfu_ref, wfd_ref, nl_ref, o_ref, *, final_norm, ff_chunks):
    D = x_ref.shape[-1]
    y_conv = _dot(hc_ref[...], wco_ref[...])
    y_attn = _dot(at_ref[...], wao_ref[...])
    g = g_ref[...].astype(jnp.float32)
    merged = g[:, :D] * y_conv + g[:, D:] * y_attn
    x1 = x_ref[...] + _dot(merged.astype(jnp.bfloat16), wo_ref[...])

    ms = jnp.mean(x1 * x1, axis=-1, keepdims=True)
    hf = (x1 * lax.rsqrt(ms + RMS_EPS) * nf_ref[...]).astype(jnp.bfloat16)
    dff = wfg_ref.shape[1]
    fc = dff // ff_chunks
    x2 = x1
    for c in range(ff_chunks):
        gt = _dot(hf, wfg_ref[:, c * fc:(c + 1) * fc])
        up = _dot(hf, wfu_ref[:, c * fc:(c + 1) * fc])
        act = (gt * _sigmoid(gt) * up).astype(jnp.bfloat16)
        x2 = x2 + _dot(act, wfd_ref[c * fc:(c + 1) * fc, :])
    if final_norm:
        ms2 = jnp.mean(x2 * x2, axis=-1, keepdims=True)
        x2 = x2 * lax.rsqrt(ms2 + RMS_EPS) * nl_ref[...]
    o_ref[...] = x2


def _post(x2d, hc, at, g, wco, wao, wo, nf, wfg, wfu, wfd, nl, final_norm):
    T, D = x2d.shape
    const = lambda i: (0, 0)
    row = lambda i: (i, 0)
    single = pl.Buffered(1)

    def wspec(a):
        return pl.BlockSpec(a.shape, const, pipeline_mode=single)

    return pl.pallas_call(
        functools.partial(_post_kernel, final_norm=final_norm, ff_chunks=2),
        grid=(T // TM_POST,),
        in_specs=[
            pl.BlockSpec((TM_POST, D), row),
            pl.BlockSpec((TM_POST, hc.shape[1]), row),
            pl.BlockSpec((TM_POST, at.shape[1]), row),
            pl.BlockSpec((TM_POST, g.shape[1]), row),
            wspec(wco), wspec(wao), wspec(wo), wspec(nf),
            wspec(wfg), wspec(wfu), wspec(wfd), wspec(nl),
        ],
        out_specs=pl.BlockSpec((TM_POST, D), row),
        out_shape=jax.ShapeDtypeStruct((T, D), jnp.float32),
        compiler_params=pltpu.CompilerParams(
            dimension_semantics=("parallel",), vmem_limit_bytes=VMEM_LIMIT),
        name="post",
    )(x2d, hc, at, g, wco, wao, wo, nf, wfg, wfu, wfd, nl)


def kernel(x, norm_mix, w_in, b_gate, conv_w, conv_b, conv_ln_g, conv_ln_b, w_conv_out,
           w_attn_out, w_out, norm_ffn, w_ffn_gate, w_ffn_up, w_ffn_down, norm_final):
    B, L, D = x.shape
    depth = w_in.shape[0]
    bf = jnp.bfloat16
    nq = N_HEADS * HEAD_DIM
    nqi = H_IDX * D_IDX
    o_q = 2 * C_CONV
    o_k = o_q + nq
    o_v = o_k + HEAD_DIM
    o_qi = o_v + HEAD_DIM
    o_ki = o_qi + nqi
    o_wi = o_ki + D_IDX
    o_g = o_wi + H_IDX

    for layer in range(depth):
        w = w_in[layer]
        wc = w[:, :o_q].astype(bf)
        wg = w[:, o_g:].astype(bf)
        wkk = jnp.concatenate([w[:, o_k:o_v], w[:, o_ki:o_wi]], axis=1).astype(bf)
        wt = jnp.concatenate([
            w[:, o_q:o_k] * (HEAD_DIM ** -0.5),
            w[:, o_qi:o_ki] * (D_IDX ** -0.5),
            w[:, o_v:o_qi],
            w[:, o_wi:o_g],
            jnp.zeros((D, 128 - HEAD_DIM - H_IDX), w.dtype),
        ], axis=1).T.astype(bf)

        hglu, gate, k, ki, qt, qit, vt, wit = _inproj(
            x, norm_mix[layer][None, :], wc, wg, b_gate[layer][None, :], wkk, wt)
        hc = _conv_branch(hglu, conv_w[layer], conv_b[layer][None, :],
                          conv_ln_g[layer][None, :], conv_ln_b[layer][None, :])
        at = _dsa(qit, wit, ki, qt, k, vt)
        x = _post(
            x.reshape(B * L, D), hc.reshape(B * L, C_CONV), at.reshape(B * L, nq),
            gate.reshape(B * L, 2 * D),
            w_conv_out[layer].astype(bf), w_attn_out[layer].astype(bf), w_out[layer].astype(bf),
            norm_ffn[layer][None, :], w_ffn_gate[layer].astype(bf), w_ffn_up[layer].astype(bf),
            w_ffn_down[layer].astype(bf), norm_final[None, :],
            final_norm=(layer == depth - 1)).reshape(B, L, D)
    return x
```

```python
import functools

import jax
import jax.numpy as jnp
from jax import lax
from jax.experimental import pallas as pl
from jax.experimental.pallas import tpu as pltpu

C_CONV = 512
CONV_WIDTH = 31
N_HEADS = 8
HEAD_DIM = 64
H_IDX = 8
D_IDX = 64
TOPK_MAX = 256
RMS_EPS = 1e-6
LN_EPS = 1e-5

V7X_VMEM_BYTES = 64 * 1024 * 1024
VMEM_LIMIT = 56 * 1024 * 1024

TM_IN = 512
TM_POST = 256
TQ = 256
CK = 256
CONV_ROWS = 256
CONV_PAD = 32
KAUG = 128
POS_SPLIT = 64
NEG_BIG = -1e30
INT_MIN = -2 ** 31

_NT = (((1,), (1,)), ((), ()))


def _alibi_slope(h):
    return 2.0 ** (-8.0 * (h + 1) / N_HEADS)


def _dot(a, b):
    return jnp.dot(a, b, preferred_element_type=jnp.float32)


def _sigmoid(x):
    return 1.0 / (1.0 + jnp.exp(-x))


def _inproj_kernel(x_ref, gn_ref, wc_ref, wg_ref, bg_ref, wkk_ref, wt_ref,
                   hglu_ref, gate_ref, k_ref, ki_ref, qt_ref, qit_ref, vt_ref, wit_ref):
    x = x_ref[0]
    ms = jnp.mean(x * x, axis=-1, keepdims=True)
    h = (x * lax.rsqrt(ms + RMS_EPS) * gn_ref[...]).astype(jnp.bfloat16)

    uc = _dot(h, wc_ref[...])
    hglu_ref[0] = uc[:, :C_CONV] * _sigmoid(uc[:, C_CONV:])

    ug = _dot(h, wg_ref[...]) + bg_ref[...]
    gate_ref[0] = _sigmoid(ug).astype(jnp.bfloat16)

    ukk = _dot(h, wkk_ref[...])
    lane = lax.broadcasted_iota(jnp.int32, (TM_IN, KAUG), 1)
    pos = pl.program_id(1) * TM_IN + lax.broadcasted_iota(jnp.int32, (TM_IN, KAUG), 0)
    feat = jnp.where(lane == HEAD_DIM, pos // POS_SPLIT,
                     jnp.where(lane == HEAD_DIM + 1, pos % POS_SPLIT, 0)).astype(jnp.float32)
    k_ref[0] = jnp.where(lane < HEAD_DIM, ukk, feat).astype(jnp.bfloat16)
    ki_ref[0] = ukk[:, HEAD_DIM:].astype(jnp.bfloat16)

    ut = lax.dot_general(wt_ref[...], h, _NT, preferred_element_type=jnp.float32)
    nq = N_HEADS * HEAD_DIM
    nqi = H_IDX * D_IDX
    qt_ref[0] = ut[:nq].astype(jnp.bfloat16)
    qit_ref[0] = ut[nq:nq + nqi].astype(jnp.bfloat16)
    vt = ut[nq + nqi:nq + nqi + HEAD_DIM].astype(jnp.bfloat16)
    for c in range(TM_IN // CK):
        vt_ref[0, c] = vt[:, c * CK:(c + 1) * CK]
    o = nq + nqi + HEAD_DIM
    wit_ref[0] = ut[o:o + H_IDX] * (H_IDX ** -0.5)


def _inproj(x, gn, wc, wg, bg, wkk, wt):
    B, L, D = x.shape
    nt = L // TM_IN
    const = lambda b, i: (0, 0)
    row = lambda b, i: (b, i, 0)
    col = lambda b, i: (b, 0, i)
    bf = jnp.bfloat16
    out_shape = (
        jax.ShapeDtypeStruct((B, L, C_CONV), jnp.float32),
        jax.ShapeDtypeStruct((B, L, 2 * D), bf),
        jax.ShapeDtypeStruct((B, L, KAUG), bf),
        jax.ShapeDtypeStruct((B, L, D_IDX), bf),
        jax.ShapeDtypeStruct((B, N_HEADS * HEAD_DIM, L), bf),
        jax.ShapeDtypeStruct((B, H_IDX * D_IDX, L), bf),
        jax.ShapeDtypeStruct((B, L // CK, HEAD_DIM, CK), bf),
        jax.ShapeDtypeStruct((B, H_IDX, L), jnp.float32),
    )
    out_specs = (
        pl.BlockSpec((1, TM_IN, C_CONV), row),
        pl.BlockSpec((1, TM_IN, 2 * D), row),
        pl.BlockSpec((1, TM_IN, KAUG), row),
        pl.BlockSpec((1, TM_IN, D_IDX), row),
        pl.BlockSpec((1, N_HEADS * HEAD_DIM, TM_IN), col),
        pl.BlockSpec((1, H_IDX * D_IDX, TM_IN), col),
        pl.BlockSpec((1, TM_IN // CK, HEAD_DIM, CK), lambda b, i: (b, i, 0, 0)),
        pl.BlockSpec((1, H_IDX, TM_IN), col),
    )
    in_specs = [
        pl.BlockSpec((1, TM_IN, D), row),
        pl.BlockSpec(gn.shape, const),
        pl.BlockSpec(wc.shape, const),
        pl.BlockSpec(wg.shape, const),
        pl.BlockSpec(bg.shape, const),
        pl.BlockSpec(wkk.shape, const),
        pl.BlockSpec(wt.shape, const),
    ]
    return pl.pallas_call(
        _inproj_kernel,
        grid=(B, nt),
        in_specs=in_specs,
        out_specs=out_specs,
        out_shape=out_shape,
        compiler_params=pltpu.CompilerParams(
            dimension_semantics=("parallel", "parallel"), vmem_limit_bytes=VMEM_LIMIT),
        name="inproj",
    )(x, gn, wc, wg, bg, wkk, wt)


def _conv_kernel(h_ref, w_ref, b_ref, lg_ref, lb_ref, o_ref, pad_ref):
    L = h_ref.shape[1]
    pad_ref[0:CONV_PAD, :] = jnp.zeros((CONV_PAD, C_CONV), jnp.float32)
    pad_ref[CONV_PAD:, :] = h_ref[0]
    shift = CONV_PAD - (CONV_WIDTH - 1)

    def tile(t, carry):
        r0 = pl.multiple_of(t * CONV_ROWS, CONV_ROWS)
        acc = jnp.zeros((CONV_ROWS, C_CONV), jnp.float32) + b_ref[...]
        win = pad_ref[pl.ds(r0, CONV_ROWS + CONV_PAD), :]
        for r in range(8):
            rolled = win if r == 0 else pltpu.roll(win, CONV_ROWS + CONV_PAD - r, axis=0)
            for k in range(CONV_WIDTH):
                if (shift + k) % 8 == r:
                    off = shift + k - r
                    acc = acc + w_ref[k:k + 1, :] * rolled[off:off + CONV_ROWS]
        mu = jnp.mean(acc, axis=-1, keepdims=True)
        xc = acc - mu
        var = jnp.mean(xc * xc, axis=-1, keepdims=True)
        y = xc * lax.rsqrt(var + LN_EPS) * lg_ref[...] + lb_ref[...]
        o_ref[0, pl.ds(r0, CONV_ROWS), :] = (y * _sigmoid(y)).astype(jnp.bfloat16)
        return carry

    lax.fori_loop(0, L // CONV_ROWS, tile, 0)


def _conv_branch(hglu, conv_w, conv_b, ln_g, ln_b):
    B, L, _ = hglu.shape
    const = lambda b: (0, 0)
    return pl.pallas_call(
        _conv_kernel,
        grid=(B,),
        in_specs=[
            pl.BlockSpec((1, L, C_CONV), lambda b: (b, 0, 0)),
            pl.BlockSpec(conv_w.shape, const),
            pl.BlockSpec(conv_b.shape, const),
            pl.BlockSpec(ln_g.shape, const),
            pl.BlockSpec(ln_b.shape, const),
        ],
        out_specs=pl.BlockSpec((1, L, C_CONV), lambda b: (b, 0, 0)),
        out_shape=jax.ShapeDtypeStruct((B, L, C_CONV), jnp.bfloat16),
        scratch_shapes=[pltpu.VMEM((CONV_PAD + L, C_CONV), jnp.float32)],
        compiler_params=pltpu.CompilerParams(
            dimension_semantics=("parallel",), vmem_limit_bytes=VMEM_LIMIT),
        name="conv_branch",
    )(hglu, conv_w, conv_b, ln_g, ln_b)


def _dsa_kernel(qit_ref, wit_ref, ki_ref, qt_ref, k_ref, vt_ref, o_ref,
                score_ref, bias_ref, ot_ref, qa_ref, m_ref, l_ref, a_ref, s_ref, p_ref,
                *, topk):
    j = pl.program_id(1)
    nck = j + 1
    t_q = j * TQ + lax.broadcasted_iota(jnp.int32, (CK, TQ), 1)
    row_iota = lax.broadcasted_iota(jnp.int32, (CK, TQ), 0)

    def score_chunk(c, carry):
        kic = ki_ref[0, pl.ds(pl.multiple_of(c * CK, CK), CK), :]
        score = jnp.zeros((CK, TQ), jnp.float32)
        for h in range(H_IDX):
            rel = _dot(kic, qit_ref[0, h * D_IDX:(h + 1) * D_IDX, :])
            score = score + wit_ref[0, h:h + 1, :] * jnp.maximum(rel, 0.0)
        causal = (c * CK + row_iota) <= t_q
        score_ref[c] = jnp.where(causal, score, -jnp.inf)
        return carry

    lax.fori_loop(0, nck, score_chunk, 0)

    def count(pred_fn):
        def body(c, acc):
            m = pred_fn(score_ref[c]).astype(jnp.int32)
            return acc + jnp.sum(m.reshape(CK // 8, 8, TQ), axis=0)
        acc = lax.fori_loop(0, nck, body, jnp.zeros((8, TQ), jnp.int32))
        return jnp.sum(acc, axis=0, keepdims=True)

    def key_to_float(key):
        return pltpu.bitcast(jnp.where(key < 0, key ^ 0x7FFFFFFF, key), jnp.float32)

    def bit_step(i, u):
        cand_u = u | jnp.left_shift(jnp.int32(1), 31 - i)
        cand = key_to_float(cand_u ^ INT_MIN)
        cnt = count(lambda sc: sc >= cand)
        return jnp.where(cnt >= topk, cand_u, u)

    u = lax.fori_loop(0, 32, bit_step, jnp.zeros((1, TQ), jnp.int32))
    few = u == 0
    thr = jnp.where(few, -jnp.inf, key_to_float(u ^ INT_MIN))
    cnt_gt = count(lambda sc: sc > thr)
    need = jnp.where(few, 0, topk - cnt_gt).astype(jnp.float32)

    tri = (lax.broadcasted_iota(jnp.int32, (CK, CK), 1)
           <= lax.broadcasted_iota(jnp.int32, (CK, CK), 0)).astype(jnp.bfloat16)

    def select_chunk(c, carry):
        sc = score_ref[c]
        eq = sc == thr
        rank = _dot(tri, jnp.where(eq, 1.0, 0.0).astype(jnp.bfloat16)) + carry
        eq_rank = jnp.where(eq, rank, 3e38)
        bias = jnp.where(sc > thr, 0.0, jnp.where(eq_rank <= need, 0.0, NEG_BIG))
        bias_ref[c] = bias
        return rank[CK - 1:CK, :]

    lax.fori_loop(0, nck, select_chunk, jnp.zeros((1, TQ), jnp.float32))

    frow = lax.broadcasted_iota(jnp.int32, (KAUG - HEAD_DIM, TQ), 0)
    for h in range(N_HEADS):
        slope = _alibi_slope(h)
        feat = jnp.where(frow == 0, slope * POS_SPLIT, jnp.where(frow == 1, slope, 0.0))
        qa_ref[h, :HEAD_DIM, :] = qt_ref[0, h * HEAD_DIM:(h + 1) * HEAD_DIM, :]
        qa_ref[h, HEAD_DIM:, :] = feat.astype(jnp.bfloat16)
    m_ref[...] = jnp.full(m_ref.shape, NEG_BIG, jnp.float32)
    l_ref[...] = jnp.zeros(l_ref.shape, jnp.float32)
    ot_ref[...] = jnp.zeros(ot_ref.shape, jnp.float32)

    def sublane_allreduce(x, op):
        for sh in (4, 2, 1):
            x = op(x, pltpu.roll(x, sh, axis=0))
        return x

    def att_chunk(c, carry):
        kc = k_ref[0, pl.ds(pl.multiple_of(c * CK, CK), CK), :]
        bias = bias_ref[c]
        vtc = vt_ref[0, c]
        for h in range(N_HEADS):
            s = _dot(kc, qa_ref[h]) + bias
            s_ref[h] = s
            mx = jnp.max(s.reshape(CK // 8, 8, TQ), axis=0)
            m_old = m_ref[h]
            m_new = jnp.maximum(m_old, sublane_allreduce(mx, jnp.maximum))
            a_ref[h] = jnp.exp(m_old - m_new)
            m_ref[h] = m_new
        for h in range(N_HEADS):
            p = jnp.exp(s_ref[h].reshape(CK // 8, 8, TQ) - m_ref[h][None])
            l_ref[h] = a_ref[h] * l_ref[h] + sublane_allreduce(jnp.sum(p, axis=0), jnp.add)
            p_ref[h] = p.reshape(CK, TQ).astype(jnp.bfloat16)
        for h in range(N_HEADS):
            pv = _dot(vtc, p_ref[h])
            rows = pl.ds(h * HEAD_DIM, HEAD_DIM)
            acc = ot_ref[rows, :].reshape(HEAD_DIM // 8, 8, TQ)
            ot_ref[rows, :] = (a_ref[h][None] * acc).reshape(HEAD_DIM, TQ) + pv
        return carry

    lax.fori_loop(0, nck, att_chunk, 0)

    for h in range(N_HEADS):
        rows = pl.ds(h * HEAD_DIM, HEAD_DIM)
        acc = ot_ref[rows, :].reshape(HEAD_DIM // 8, 8, TQ)
        ot_ref[rows, :] = (acc / l_ref[h][None]).reshape(HEAD_DIM, TQ)
    o_ref[0] = ot_ref[...].T.astype(jnp.bfloat16)


def _dsa(qit, wit, ki, qt, k, vt):
    B, _, L = qt.shape
    nq = L // TQ
    nck = L // CK
    topk = min(TOPK_MAX, L // 4)
    per_b = lambda b, j: (b, 0, 0)
    col = lambda b, j: (b, 0, j)
    return pl.pallas_call(
        functools.partial(_dsa_kernel, topk=topk),
        grid=(B, nq),
        in_specs=[
            pl.BlockSpec((1, H_IDX * D_IDX, TQ), col),
            pl.BlockSpec((1, H_IDX, TQ), col),
            pl.BlockSpec((1, L, D_IDX), per_b),
            pl.BlockSpec((1, N_HEADS * HEAD_DIM, TQ), col),
            pl.BlockSpec((1, L, KAUG), per_b),
            pl.BlockSpec((1, nck, HEAD_DIM, CK), lambda b, j: (b, 0, 0, 0)),
        ],
        out_specs=pl.BlockSpec((1, TQ, N_HEADS * HEAD_DIM), lambda b, j: (b, j, 0)),
        out_shape=jax.ShapeDtypeStruct((B, L, N_HEADS * HEAD_DIM), jnp.bfloat16),
        scratch_shapes=[
            pltpu.VMEM((nck, CK, TQ), jnp.float32),
            pltpu.VMEM((nck, CK, TQ), jnp.float32),
            pltpu.VMEM((N_HEADS * HEAD_DIM, TQ), jnp.float32),
            pltpu.VMEM((N_HEADS, KAUG, TQ), jnp.bfloat16),
            pltpu.VMEM((N_HEADS, 8, TQ), jnp.float32),
            pltpu.VMEM((N_HEADS, 8, TQ), jnp.float32),
            pltpu.VMEM((N_HEADS, 8, TQ), jnp.float32),
            pltpu.VMEM((N_HEADS, CK, TQ), jnp.float32),
            pltpu.VMEM((N_HEADS, CK, TQ), jnp.bfloat16),
        ],
        compiler_params=pltpu.CompilerParams(
            dimension_semantics=("parallel", "parallel"), vmem_limit_bytes=VMEM_LIMIT),
        name="dsa_attention",
    )(qit, wit, ki, qt, k, vt)


def _post_kernel(x_ref, hc_ref, at_ref, g_ref, wco_ref, wao_ref, wo_ref, nf_ref,
                 wfg_ref, wfu_ref, wfd_ref, nl_ref, o_ref, *, final_norm, ff_chunks):
    D = x_ref.shape[-1]
    y_conv = _dot(hc_ref[...], wco_ref[...])
    y_attn = _dot(at_ref[...], wao_ref[...])
    g = g_ref[...].astype(jnp.float32)
    merged = g[:, :D] * y_conv + g[:, D:] * y_attn
    x1 = x_ref[...] + _dot(merged.astype(jnp.bfloat16), wo_ref[...])

    ms = jnp.mean(x1 * x1, axis=-1, keepdims=True)
    hf = (x1 * lax.rsqrt(ms + RMS_EPS) * nf_ref[...]).astype(jnp.bfloat16)
    dff = wfg_ref.shape[1]
    fc = dff // ff_chunks
    x2 = x1
    for c in range(ff_chunks):
        gt = _dot(hf, wfg_ref[:, c * fc:(c + 1) * fc])
        up = _dot(hf, wfu_ref[:, c * fc:(c + 1) * fc])
        act = (gt * _sigmoid(gt) * up).astype(jnp.bfloat16)
        x2 = x2 + _dot(act, wfd_ref[c * fc:(c + 1) * fc, :])
    if final_norm:
        ms2 = jnp.mean(x2 * x2, axis=-1, keepdims=True)
        x2 = x2 * lax.rsqrt(ms2 + RMS_EPS) * nl_ref[...]
    o_ref[...] = x2


def _post(x2d, hc, at, g, wco, wao, wo, nf, wfg, wfu, wfd, nl, final_norm):
    T, D = x2d.shape
    const = lambda i: (0, 0)
    row = lambda i: (i, 0)
    single = pl.Buffered(1)

    def wspec(a):
        return pl.BlockSpec(a.shape, const, pipeline_mode=single)

    return pl.pallas_call(
        functools.partial(_post_kernel, final_norm=final_norm, ff_chunks=2),
        grid=(T // TM_POST,),
        in_specs=[
            pl.BlockSpec((TM_POST, D), row),
            pl.BlockSpec((TM_POST, hc.shape[1]), row),
            pl.BlockSpec((TM_POST, at.shape[1]), row),
            pl.BlockSpec((TM_POST, g.shape[1]), row),
            wspec(wco), wspec(wao), wspec(wo), wspec(nf),
            wspec(wfg), wspec(wfu), wspec(wfd), wspec(nl),
        ],
        out_specs=pl.BlockSpec((TM_POST, D), row),
        out_shape=jax.ShapeDtypeStruct((T, D), jnp.float32),
        compiler_params=pltpu.CompilerParams(
            dimension_semantics=("parallel",), vmem_limit_bytes=VMEM_LIMIT),
        name="post",
    )(x2d, hc, at, g, wco, wao, wo, nf, wfg, wfu, wfd, nl)


def kernel(x, norm_mix, w_in, b_gate, conv_w, conv_b, conv_ln_g, conv_ln_b, w_conv_out,
           w_attn_out, w_out, norm_ffn, w_ffn_gate, w_ffn_up, w_ffn_down, norm_final):
    B, L, D = x.shape
    depth = w_in.shape[0]
    bf = jnp.bfloat16
    nq = N_HEADS * HEAD_DIM
    nqi = H_IDX * D_IDX
    o_q = 2 * C_CONV
    o_k = o_q + nq
    o_v = o_k + HEAD_DIM
    o_qi = o_v + HEAD_DIM
    o_ki = o_qi + nqi
    o_wi = o_ki + D_IDX
    o_g = o_wi + H_IDX

    for layer in range(depth):
        w = w_in[layer]
        wc = w[:, :o_q].astype(bf)
        wg = w[:, o_g:].astype(bf)
        wkk = jnp.concatenate([w[:, o_k:o_v], w[:, o_ki:o_wi]], axis=1).astype(bf)
        wt = jnp.concatenate([
            w[:, o_q:o_k] * (HEAD_DIM ** -0.5),
            w[:, o_qi:o_ki] * (D_IDX ** -0.5),
            w[:, o_v:o_qi],
            w[:, o_wi:o_g],
            jnp.zeros((D, 128 - HEAD_DIM - H_IDX), w.dtype),
        ], axis=1).T.astype(bf)

        hglu, gate, k, ki, qt, qit, vt, wit = _inproj(
            x, norm_mix[layer][None, :], wc, wg, b_gate[layer][None, :], wkk, wt)
        hc = _conv_branch(hglu, conv_w[layer], conv_b[layer][None, :],
                          conv_ln_g[layer][None, :], conv_ln_b[layer][None, :])
        at = _dsa(qit, wit, ki, qt, k, vt)
        x = _post(
            x.reshape(B * L, D), hc.reshape(B * L, C_CONV), at.reshape(B * L, nq),
            gate.reshape(B * L, 2 * D),
            w_conv_out[layer].astype(bf), w_attn_out[layer].astype(bf), w_out[layer].astype(bf),
            norm_ffn[layer][None, :], w_ffn_gate[layer].astype(bf), w_ffn_up[layer].astype(bf),
            w_ffn_down[layer].astype(bf), norm_final[None, :],
            final_norm=(layer == depth - 1)).reshape(B, L, D)
    return x
```

```python
import functools

import jax
import jax.numpy as jnp
from jax import lax
from jax.experimental import pallas as pl
from jax.experimental.pallas import tpu as pltpu

C_CONV = 512
CONV_WIDTH = 31
N_HEADS = 8
HEAD_DIM = 64
H_IDX = 8
D_IDX = 64
TOPK_MAX = 256
RMS_EPS = 1e-6
LN_EPS = 1e-5

V7X_VMEM_BYTES = 64 * 1024 * 1024
VMEM_LIMIT = 56 * 1024 * 1024

TM_IN = 512
TM_POST = 256
TQ = 256
CK = 256
CONV_ROWS = 256
CONV_PAD = 32
KAUG = 128
POS_SPLIT = 64
NEG_BIG = -1e30
INT_MIN = -2 ** 31

_NT = (((1,), (1,)), ((), ()))


def _alibi_slope(h):
    return 2.0 ** (-8.0 * (h + 1) / N_HEADS)


def _dot(a, b):
    return jnp.dot(a, b, preferred_element_type=jnp.float32)


def _sigmoid(x):
    return 1.0 / (1.0 + jnp.exp(-x))


def _inproj_kernel(x_ref, gn_ref, wc_ref, wg_ref, bg_ref, wkk_ref, wt_ref,
                   hglu_ref, gate_ref, k_ref, ki_ref, qt_ref, qit_ref, vt_ref, wit_ref):
    x = x_ref[0]
    ms = jnp.mean(x * x, axis=-1, keepdims=True)
    h = (x * lax.rsqrt(ms + RMS_EPS) * gn_ref[...]).astype(jnp.bfloat16)

    uc = _dot(h, wc_ref[...])
    hglu_ref[0] = uc[:, :C_CONV] * _sigmoid(uc[:, C_CONV:])

    ug = _dot(h, wg_ref[...]) + bg_ref[...]
    gate_ref[0] = _sigmoid(ug).astype(jnp.bfloat16)

    ukk = _dot(h, wkk_ref[...])
    lane = lax.broadcasted_iota(jnp.int32, (TM_IN, KAUG), 1)
    pos = pl.program_id(1) * TM_IN + lax.broadcasted_iota(jnp.int32, (TM_IN, KAUG), 0)
    feat = jnp.where(lane == HEAD_DIM, pos // POS_SPLIT,
                     jnp.where(lane == HEAD_DIM + 1, pos % POS_SPLIT, 0)).astype(jnp.float32)
    k_ref[0] = jnp.where(lane < HEAD_DIM, ukk, feat).astype(jnp.bfloat16)
    ki_ref[0] = ukk[:, HEAD_DIM:].astype(jnp.bfloat16)

    ut = lax.dot_general(wt_ref[...], h, _NT, preferred_element_type=jnp.float32)
    nq = N_HEADS * HEAD_DIM
    nqi = H_IDX * D_IDX
    qt_ref[0] = ut[:nq].astype(jnp.bfloat16)
    qit_ref[0] = ut[nq:nq + nqi].astype(jnp.bfloat16)
    vt = ut[nq + nqi:nq + nqi + HEAD_DIM].astype(jnp.bfloat16)
    for c in range(TM_IN // CK):
        vt_ref[0, c] = vt[:, c * CK:(c + 1) * CK]
    o = nq + nqi + HEAD_DIM
    wit_ref[0] = ut[o:o + H_IDX] * (H_IDX ** -0.5)


def _inproj(x, gn, wc, wg, bg, wkk, wt):
    B, L, D = x.shape
    nt = L // TM_IN
    const = lambda b, i: (0, 0)
    row = lambda b, i: (b, i, 0)
    col = lambda b, i: (b, 0, i)
    bf = jnp.bfloat16
    out_shape = (
        jax.ShapeDtypeStruct((B, L, C_CONV), jnp.float32),
        jax.ShapeDtypeStruct((B, L, 2 * D), bf),
        jax.ShapeDtypeStruct((B, L, KAUG), bf),
        jax.ShapeDtypeStruct((B, L, D_IDX), bf),
        jax.ShapeDtypeStruct((B, N_HEADS * HEAD_DIM, L), bf),
        jax.ShapeDtypeStruct((B, H_IDX * D_IDX, L), bf),
        jax.ShapeDtypeStruct((B, L // CK, HEAD_DIM, CK), bf),
        jax.ShapeDtypeStruct((B, H_IDX, L), jnp.float32),
    )
    out_specs = (
        pl.BlockSpec((1, TM_IN, C_CONV), row),
        pl.BlockSpec((1, TM_IN, 2 * D), row),
        pl.BlockSpec((1, TM_IN, KAUG), row),
        pl.BlockSpec((1, TM_IN, D_IDX), row),
        pl.BlockSpec((1, N_HEADS * HEAD_DIM, TM_IN), col),
        pl.BlockSpec((1, H_IDX * D_IDX, TM_IN), col),
        pl.BlockSpec((1, TM_IN // CK, HEAD_DIM, CK), lambda b, i: (b, i, 0, 0)),
        pl.BlockSpec((1, H_IDX, TM_IN), col),
    )
    in_specs = [
        pl.BlockSpec((1, TM_IN, D), row),
        pl.BlockSpec(gn.shape, const),
        pl.BlockSpec(wc.shape, const),
        pl.BlockSpec(wg.shape, const),
        pl.BlockSpec(bg.shape, const),
        pl.BlockSpec(wkk.shape, const),
        pl.BlockSpec(wt.shape, const),
    ]
    return pl.pallas_call(
        _inproj_kernel,
        grid=(B, nt),
        in_specs=in_specs,
        out_specs=out_specs,
        out_shape=out_shape,
        compiler_params=pltpu.CompilerParams(
            dimension_semantics=("parallel", "parallel"), vmem_limit_bytes=VMEM_LIMIT),
        name="inproj",
    )(x, gn, wc, wg, bg, wkk, wt)


def _conv_kernel(h_ref, w_ref, b_ref, lg_ref, lb_ref, o_ref, pad_ref):
    L = h_ref.shape[1]
    pad_ref[0:CONV_PAD, :] = jnp.zeros((CONV_PAD, C_CONV), jnp.float32)
    pad_ref[CONV_PAD:, :] = h_ref[0]
    shift = CONV_PAD - (CONV_WIDTH - 1)

    def tile(t, carry):
        r0 = pl.multiple_of(t * CONV_ROWS, CONV_ROWS)
        acc = jnp.zeros((CONV_ROWS, C_CONV), jnp.float32) + b_ref[...]
        win = pad_ref[pl.ds(r0, CONV_ROWS + CONV_PAD), :]
        for r in range(8):
            rolled = win if r == 0 else pltpu.roll(win, CONV_ROWS + CONV_PAD - r, axis=0)
            for k in range(CONV_WIDTH):
                if (shift + k) % 8 == r:
                    off = shift + k - r
                    acc = acc + w_ref[k:k + 1, :] * rolled[off:off + CONV_ROWS]
        mu = jnp.mean(acc, axis=-1, keepdims=True)
        xc = acc - mu
        var = jnp.mean(xc * xc, axis=-1, keepdims=True)
        y = xc * lax.rsqrt(var + LN_EPS) * lg_ref[...] + lb_ref[...]
        o_ref[0, pl.ds(r0, CONV_ROWS), :] = (y * _sigmoid(y)).astype(jnp.bfloat16)
        return carry

    lax.fori_loop(0, L // CONV_ROWS, tile, 0)


def _conv_branch(hglu, conv_w, conv_b, ln_g, ln_b):
    B, L, _ = hglu.shape
    const = lambda b: (0, 0)
    return pl.pallas_call(
        _conv_kernel,
        grid=(B,),
        in_specs=[
            pl.BlockSpec((1, L, C_CONV), lambda b: (b, 0, 0)),
            pl.BlockSpec(conv_w.shape, const),
            pl.BlockSpec(conv_b.shape, const),
            pl.BlockSpec(ln_g.shape, const),
            pl.BlockSpec(ln_b.shape, const),
        ],
        out_specs=pl.BlockSpec((1, L, C_CONV), lambda b: (b, 0, 0)),
        out_shape=jax.ShapeDtypeStruct((B, L, C_CONV), jnp.bfloat16),
        scratch_shapes=[pltpu.VMEM((CONV_PAD + L, C_CONV), jnp.float32)],
        compiler_params=pltpu.CompilerParams(
            dimension_semantics=("parallel",), vmem_limit_bytes=VMEM_LIMIT),
        name="conv_branch",
    )(hglu, conv_w, conv_b, ln_g, ln_b)


def _dsa_kernel(qit_ref, wit_ref, ki_ref, qt_ref, k_ref, vt_ref, o_ref,
                score_ref, bias_ref, ot_ref, qa_ref, m_ref, l_ref, a_ref, s_ref, p_ref,
                *, topk):
    j = pl.program_id(1)
    nck = j + 1
    t_q = j * TQ + lax.broadcasted_iota(jnp.int32, (CK, TQ), 1)
    row_iota = lax.broadcasted_iota(jnp.int32, (CK, TQ), 0)

    def score_chunk(c, carry):
        kic = ki_ref[0, pl.ds(pl.multiple_of(c * CK, CK), CK), :]
        score = jnp.zeros((CK, TQ), jnp.float32)
        for h in range(H_IDX):
            rel = _dot(kic, qit_ref[0, h * D_IDX:(h + 1) * D_IDX, :])
            score = score + wit_ref[0, h:h + 1, :] * jnp.maximum(rel, 0.0)
        causal = (c * CK + row_iota) <= t_q
        score_ref[c] = jnp.where(causal, score, -jnp.inf)
        return carry

    lax.fori_loop(0, nck, score_chunk, 0)

    def sublane_allreduce(x, op):
        for sh in (4, 2, 1):
            x = op(x, pltpu.roll(x, sh, axis=0))
        return x

    def count(pred_fn):
        def body(c, acc):
            m = pred_fn(score_ref[c].reshape(CK // 8, 8, TQ)).astype(jnp.int32)
            return acc + jnp.sum(m.reshape(CK // 64, 8, 8, TQ), axis=0)
        acc = lax.fori_loop(0, nck, body, jnp.zeros((8, 8, TQ), jnp.int32))
        return sublane_allreduce(jnp.sum(acc, axis=0), jnp.add)

    def key_to_float(key):
        return pltpu.bitcast(jnp.where(key < 0, key ^ 0x7FFFFFFF, key), jnp.float32)

    def bit_step(i, u):
        cand_u = u | jnp.left_shift(jnp.int32(1), 31 - i)
        cand = key_to_float(cand_u ^ INT_MIN)
        cnt = count(lambda sc: sc >= cand[None])
        return jnp.where(cnt >= topk, cand_u, u)

    u = lax.fori_loop(0, 32, bit_step, jnp.zeros((8, TQ), jnp.int32))
    few = u == 0
    thr = jnp.where(few, -jnp.inf, key_to_float(u ^ INT_MIN))
    cnt_ge = count(lambda sc: sc >= thr[None])
    has_tie = jnp.logical_and(cnt_ge > topk, jnp.logical_not(few))
    any_tie = jnp.max(has_tie.astype(jnp.int32)) > 0

    def select_plain():
        thr_fin = jnp.where(few, jnp.finfo(jnp.float32).min, thr)[None]

        def body(c, carry):
            sc = score_ref[c].reshape(CK // 8, 8, TQ)
            bias_ref[c] = jnp.where(sc >= thr_fin, 0.0, NEG_BIG).reshape(CK, TQ)
            return carry

        lax.fori_loop(0, nck, body, 0)

    def select_ties():
        cnt_gt = count(lambda sc: sc > thr[None])
        need = jnp.where(few, 0, topk - cnt_gt).astype(jnp.float32)[None]
        tri = (lax.broadcasted_iota(jnp.int32, (CK, CK), 1)
               <= lax.broadcasted_iota(jnp.int32, (CK, CK), 0)).astype(jnp.bfloat16)

        def body(c, carry):
            sc = score_ref[c].reshape(CK // 8, 8, TQ)
            eq = sc == thr[None]
            eq01 = jnp.where(eq, 1.0, 0.0).reshape(CK, TQ).astype(jnp.bfloat16)
            rank = _dot(tri, eq01) + carry
            eq_rank = jnp.where(eq, rank.reshape(CK // 8, 8, TQ), 3e38)
            bias = jnp.where(sc > thr[None], 0.0, jnp.where(eq_rank <= need, 0.0, NEG_BIG))
            bias_ref[c] = bias.reshape(CK, TQ)
            return rank[CK - 1:CK, :]

        lax.fori_loop(0, nck, body, jnp.zeros((1, TQ), jnp.float32))

    lax.cond(any_tie, select_ties, select_plain)

    frow = lax.broadcasted_iota(jnp.int32, (KAUG - HEAD_DIM, TQ), 0)
    for h in range(N_HEADS):
        slope = _alibi_slope(h)
        feat = jnp.where(frow == 0, slope * POS_SPLIT, jnp.where(frow == 1, slope, 0.0))
        qa_ref[h, :HEAD_DIM, :] = qt_ref[0, h * HEAD_DIM:(h + 1) * HEAD_DIM, :]
        qa_ref[h, HEAD_DIM:, :] = feat.astype(jnp.bfloat16)
    m_ref[...] = jnp.full(m_ref.shape, NEG_BIG, jnp.float32)
    l_ref[...] = jnp.zeros(l_ref.shape, jnp.float32)
    ot_ref[...] = jnp.zeros(ot_ref.shape, jnp.float32)

    def att_chunk(c, carry):
        kc = k_ref[0, pl.ds(pl.multiple_of(c * CK, CK), CK), :]
        bias = bias_ref[c]
        vtc = vt_ref[0, c]
        for h in range(N_HEADS):
            s = _dot(kc, qa_ref[h]) + bias
            s_ref[h] = s
            mx = jnp.max(s.reshape(CK // 8, 8, TQ), axis=0)
            m_old = m_ref[h]
            m_new = jnp.maximum(m_old, sublane_allreduce(mx, jnp.maximum))
            a_ref[h] = jnp.exp(m_old - m_new)
            m_ref[h] = m_new
        for h in range(N_HEADS):
            p = jnp.exp(s_ref[h].reshape(CK // 8, 8, TQ) - m_ref[h][None])
            l_ref[h] = a_ref[h] * l_ref[h] + sublane_allreduce(jnp.sum(p, axis=0), jnp.add)
            p_ref[h] = p.reshape(CK, TQ).astype(jnp.bfloat16)
        for h in range(N_HEADS):
            pv = _dot(vtc, p_ref[h])
            rows = pl.ds(h * HEAD_DIM, HEAD_DIM)
            acc = ot_ref[rows, :].reshape(HEAD_DIM // 8, 8, TQ)
            ot_ref[rows, :] = (a_ref[h][None] * acc).reshape(HEAD_DIM, TQ) + pv
        return carry

    lax.fori_loop(0, nck, att_chunk, 0)

    for h in range(N_HEADS):
        rows = pl.ds(h * HEAD_DIM, HEAD_DIM)
        acc = ot_ref[rows, :].reshape(HEAD_DIM // 8, 8, TQ)
        ot_ref[rows, :] = (acc / l_ref[h][None]).reshape(HEAD_DIM, TQ)
    o_ref[0] = ot_ref[...].T.astype(jnp.bfloat16)


def _dsa(qit, wit, ki, qt, k, vt):
    B, _, L = qt.shape
    nq = L // TQ
    nck = L // CK
    topk = min(TOPK_MAX, L // 4)
    per_b = lambda b, j: (b, 0, 0)
    col = lambda b, j: (b, 0, j)
    return pl.pallas_call(
        functools.partial(_dsa_kernel, topk=topk),
        grid=(B, nq),
        in_specs=[
            pl.BlockSpec((1, H_IDX * D_IDX, TQ), col),
            pl.BlockSpec((1, H_IDX, TQ), col),
            pl.BlockSpec((1, L, D_IDX), per_b),
            pl.BlockSpec((1, N_HEADS * HEAD_DIM, TQ), col),
            pl.BlockSpec((1, L, KAUG), per_b),
            pl.BlockSpec((1, nck, HEAD_DIM, CK), lambda b, j: (b, 0, 0, 0)),
        ],
        out_specs=pl.BlockSpec((1, TQ, N_HEADS * HEAD_DIM), lambda b, j: (b, j, 0)),
        out_shape=jax.ShapeDtypeStruct((B, L, N_HEADS * HEAD_DIM), jnp.bfloat16),
        scratch_shapes=[
            pltpu.VMEM((nck, CK, TQ), jnp.float32),
            pltpu.VMEM((nck, CK, TQ), jnp.float32),
            pltpu.VMEM((N_HEADS * HEAD_DIM, TQ), jnp.float32),
            pltpu.VMEM((N_HEADS, KAUG, TQ), jnp.bfloat16),
            pltpu.VMEM((N_HEADS, 8, TQ), jnp.float32),
            pltpu.VMEM((N_HEADS, 8, TQ), jnp.float32),
            pltpu.VMEM((N_HEADS, 8, TQ), jnp.float32),
            pltpu.VMEM((N_HEADS, CK, TQ), jnp.float32),
            pltpu.VMEM((N_HEADS, CK, TQ), jnp.bfloat16),
        ],
        compiler_params=pltpu.CompilerParams(
            dimension_semantics=("parallel", "parallel"), vmem_limit_bytes=VMEM_LIMIT),
        name="dsa_attention",
    )(qit, wit, ki, qt, k, vt)


def _post_kernel(x_ref, hc_ref, at_ref, g_ref, wco_ref, wao_ref, wo_ref, nf_ref,
                 wfg_ref, wfu_ref, wfd_ref, nl_ref, o_ref, *, final_norm, ff_chunks):
    D = x_ref.shape[-1]
    y_conv = _dot(hc_ref[...], wco_ref[...])
    y_attn = _dot(at_ref[...], wao_ref[...])
    g = g_ref[...].astype(jnp.float32)
    merged = g[:, :D] * y_conv + g[:, D:] * y_attn
    x1 = x_ref[...] + _dot(merged.astype(jnp.bfloat16), wo_ref[...])

    ms = jnp.mean(x1 * x1, axis=-1, keepdims=True)
    hf = (x1 * lax.rsqrt(ms + RMS_EPS) * nf_ref[...]).astype(jnp.bfloat16)
    dff = wfg_ref.shape[1]
    fc = dff // ff_chunks
    x2 = x1
    for c in range(ff_chunks):
        gt = _dot(hf, wfg_ref[:, c * fc:(c + 1) * fc])
        up = _dot(hf, wfu_ref[:, c * fc:(c + 1) * fc])
        act = (gt * _sigmoid(gt) * up).astype(jnp.bfloat16)
        x2 = x2 + _dot(act, wfd_ref[c * fc:(c + 1) * fc, :])
    if final_norm:
        ms2 = jnp.mean(x2 * x2, axis=-1, keepdims=True)
        x2 = x2 * lax.rsqrt(ms2 + RMS_EPS) * nl_ref[...]
    o_ref[...] = x2


def _post(x2d, hc, at, g, wco, wao, wo, nf, wfg, wfu, wfd, nl, final_norm):
    T, D = x2d.shape
    const = lambda i: (0, 0)
    row = lambda i: (i, 0)
    single = pl.Buffered(1)

    def wspec(a):
        return pl.BlockSpec(a.shape, const, pipeline_mode=single)

    return pl.pallas_call(
        functools.partial(_post_kernel, final_norm=final_norm, ff_chunks=2),
        grid=(T // TM_POST,),
        in_specs=[
            pl.BlockSpec((TM_POST, D), row),
            pl.BlockSpec((TM_POST, hc.shape[1]), row),
            pl.BlockSpec((TM_POST, at.shape[1]), row),
            pl.BlockSpec((TM_POST, g.shape[1]), row),
            wspec(wco), wspec(wao), wspec(wo), wspec(nf),
            wspec(wfg), wspec(wfu), wspec(wfd), wspec(nl),
        ],
        out_specs=pl.BlockSpec((TM_POST, D), row),
        out_shape=jax.ShapeDtypeStruct((T, D), jnp.float32),
        compiler_params=pltpu.CompilerParams(
            dimension_semantics=("parallel",), vmem_limit_bytes=VMEM_LIMIT),
        name="post",
    )(x2d, hc, at, g, wco, wao, wo, nf, wfg, wfu, wfd, nl)


def kernel(x, norm_mix, w_in, b_gate, conv_w, conv_b, conv_ln_g, conv_ln_b, w_conv_out,
           w_attn_out, w_out, norm_ffn, w_ffn_gate, w_ffn_up, w_ffn_down, norm_final):
    B, L, D = x.shape
    depth = w_in.shape[0]
    bf = jnp.bfloat16
    nq = N_HEADS * HEAD_DIM
    nqi = H_IDX * D_IDX
    o_q = 2 * C_CONV
    o_k = o_q + nq
    o_v = o_k + HEAD_DIM
    o_qi = o_v + HEAD_DIM
    o_ki = o_qi + nqi
    o_wi = o_ki + D_IDX
    o_g = o_wi + H_IDX

    for layer in range(depth):
        w = w_in[layer]
        wc = w[:, :o_q].astype(bf)
        wg = w[:, o_g:].astype(bf)
        wkk = jnp.concatenate([w[:, o_k:o_v], w[:, o_ki:o_wi]], axis=1).astype(bf)
        wt = jnp.concatenate([
            w[:, o_q:o_k] * (HEAD_DIM ** -0.5),
            w[:, o_qi:o_ki] * (D_IDX ** -0.5),
            w[:, o_v:o_qi],
            w[:, o_wi:o_g],
            jnp.zeros((D, 128 - HEAD_DIM - H_IDX), w.dtype),
        ], axis=1).T.astype(bf)

        hglu, gate, k, ki, qt, qit, vt, wit = _inproj(
            x, norm_mix[layer][None, :], wc, wg, b_gate[layer][None, :], wkk, wt)
        hc = _conv_branch(hglu, conv_w[layer], conv_b[layer][None, :],
                          conv_ln_g[layer][None, :], conv_ln_b[layer][None, :])
        at = _dsa(qit, wit, ki, qt, k, vt)
        x = _post(
            x.reshape(B * L, D), hc.reshape(B * L, C_CONV), at.reshape(B * L, nq),
            gate.reshape(B * L, 2 * D),
            w_conv_out[layer].astype(bf), w_attn_out[layer].astype(bf), w_out[layer].astype(bf),
            norm_ffn[layer][None, :], w_ffn_gate[layer].astype(bf), w_ffn_up[layer].astype(bf),
            w_ffn_down[layer].astype(bf), norm_final[None, :],
            final_norm=(layer == depth - 1)).reshape(B, L, D)
    return x
```

```python
import functools

import jax
import jax.numpy as jnp
from jax import lax
from jax.experimental import pallas as pl
from jax.experimental.pallas import tpu as pltpu

C_CONV = 512
CONV_WIDTH = 31
N_HEADS = 8
HEAD_DIM = 64
H_IDX = 8
D_IDX = 64
TOPK_MAX = 256
RMS_EPS = 1e-6
LN_EPS = 1e-5

V7X_VMEM_BYTES = 64 * 1024 * 1024
VMEM_LIMIT = 56 * 1024 * 1024

TM_IN = 512
TM_POST = 256
TQ = 256
CK = 256
CONV_ROWS = 256
CONV_PAD = 32
KAUG = 128
POS_SPLIT = 64
VAUG = 80
NEG_BIG = -1e30
INT_MIN = -2 ** 31

_NT = (((1,), (1,)), ((), ()))


def _alibi_slope(h):
    return 2.0 ** (-8.0 * (h + 1) / N_HEADS)


def _dot(a, b):
    return jnp.dot(a, b, preferred_element_type=jnp.float32)


def _sigmoid(x):
    return 1.0 / (1.0 + jnp.exp(-x))


def _inproj_kernel(x_ref, gn_ref, wc_ref, wg_ref, bg_ref, wkk_ref, wt_ref,
                   hglu_ref, gate_ref, k_ref, ki_ref, qt_ref, qit_ref, vt_ref, wit_ref):
    x = x_ref[0]
    ms = jnp.mean(x * x, axis=-1, keepdims=True)
    h = (x * lax.rsqrt(ms + RMS_EPS) * gn_ref[...]).astype(jnp.bfloat16)

    uc = _dot(h, wc_ref[...])
    hglu_ref[0] = uc[:, :C_CONV] * _sigmoid(uc[:, C_CONV:])

    ug = _dot(h, wg_ref[...]) + bg_ref[...]
    gate_ref[0] = _sigmoid(ug).astype(jnp.bfloat16)

    ukk = _dot(h, wkk_ref[...])
    lane = lax.broadcasted_iota(jnp.int32, (TM_IN, KAUG), 1)
    pos = pl.program_id(1) * TM_IN + lax.broadcasted_iota(jnp.int32, (TM_IN, KAUG), 0)
    feat = jnp.where(lane == HEAD_DIM, pos // POS_SPLIT,
                     jnp.where(lane == HEAD_DIM + 1, pos % POS_SPLIT, 0)).astype(jnp.float32)
    k_ref[0] = jnp.where(lane < HEAD_DIM, ukk, feat).astype(jnp.bfloat16)
    ki_ref[0] = ukk[:, HEAD_DIM:].astype(jnp.bfloat16)

    ut = lax.dot_general(wt_ref[...], h, _NT, preferred_element_type=jnp.float32)
    nq = N_HEADS * HEAD_DIM
    nqi = H_IDX * D_IDX
    qt_ref[0] = ut[:nq].astype(jnp.bfloat16)
    qit_ref[0] = ut[nq:nq + nqi].astype(jnp.bfloat16)
    vt = ut[nq + nqi:nq + nqi + HEAD_DIM].astype(jnp.bfloat16)
    ones_rows = (lax.broadcasted_iota(jnp.int32, (VAUG - HEAD_DIM, CK), 0) < 8).astype(jnp.bfloat16)
    for c in range(TM_IN // CK):
        vt_ref[0, c, :HEAD_DIM, :] = vt[:, c * CK:(c + 1) * CK]
        vt_ref[0, c, HEAD_DIM:, :] = ones_rows
    o = nq + nqi + HEAD_DIM
    wit_ref[0] = ut[o:o + H_IDX] * (H_IDX ** -0.5)


def _inproj(x, gn, wc, wg, bg, wkk, wt):
    B, L, D = x.shape
    nt = L // TM_IN
    const = lambda b, i: (0, 0)
    row = lambda b, i: (b, i, 0)
    col = lambda b, i: (b, 0, i)
    bf = jnp.bfloat16
    out_shape = (
        jax.ShapeDtypeStruct((B, L, C_CONV), jnp.float32),
        jax.ShapeDtypeStruct((B, L, 2 * D), bf),
        jax.ShapeDtypeStruct((B, L, KAUG), bf),
        jax.ShapeDtypeStruct((B, L, D_IDX), bf),
        jax.ShapeDtypeStruct((B, N_HEADS * HEAD_DIM, L), bf),
        jax.ShapeDtypeStruct((B, H_IDX * D_IDX, L), bf),
        jax.ShapeDtypeStruct((B, L // CK, VAUG, CK), bf),
        jax.ShapeDtypeStruct((B, H_IDX, L), jnp.float32),
    )
    out_specs = (
        pl.BlockSpec((1, TM_IN, C_CONV), row),
        pl.BlockSpec((1, TM_IN, 2 * D), row),
        pl.BlockSpec((1, TM_IN, KAUG), row),
        pl.BlockSpec((1, TM_IN, D_IDX), row),
        pl.BlockSpec((1, N_HEADS * HEAD_DIM, TM_IN), col),
        pl.BlockSpec((1, H_IDX * D_IDX, TM_IN), col),
        pl.BlockSpec((1, TM_IN // CK, VAUG, CK), lambda b, i: (b, i, 0, 0)),
        pl.BlockSpec((1, H_IDX, TM_IN), col),
    )
    in_specs = [
        pl.BlockSpec((1, TM_IN, D), row),
        pl.BlockSpec(gn.shape, const),
        pl.BlockSpec(wc.shape, const),
        pl.BlockSpec(wg.shape, const),
        pl.BlockSpec(bg.shape, const),
        pl.BlockSpec(wkk.shape, const),
        pl.BlockSpec(wt.shape, const),
    ]
    return pl.pallas_call(
        _inproj_kernel,
        grid=(B, nt),
        in_specs=in_specs,
        out_specs=out_specs,
        out_shape=out_shape,
        compiler_params=pltpu.CompilerParams(
            dimension_semantics=("parallel", "parallel"), vmem_limit_bytes=VMEM_LIMIT),
        name="inproj",
    )(x, gn, wc, wg, bg, wkk, wt)


def _conv_kernel(h_ref, w_ref, b_ref, lg_ref, lb_ref, o_ref, pad_ref):
    L = h_ref.shape[1]
    pad_ref[0:CONV_PAD, :] = jnp.zeros((CONV_PAD, C_CONV), jnp.float32)
    pad_ref[CONV_PAD:, :] = h_ref[0]
    shift = CONV_PAD - (CONV_WIDTH - 1)

    def tile(t, carry):
        r0 = pl.multiple_of(t * CONV_ROWS, CONV_ROWS)
        acc = jnp.zeros((CONV_ROWS, C_CONV), jnp.float32) + b_ref[...]
        win = pad_ref[pl.ds(r0, CONV_ROWS + CONV_PAD), :]
        for r in range(8):
            rolled = win if r == 0 else pltpu.roll(win, CONV_ROWS + CONV_PAD - r, axis=0)
            for k in range(CONV_WIDTH):
                if (shift + k) % 8 == r:
                    off = shift + k - r
                    acc = acc + w_ref[k:k + 1, :] * rolled[off:off + CONV_ROWS]
        mu = jnp.mean(acc, axis=-1, keepdims=True)
        xc = acc - mu
        var = jnp.mean(xc * xc, axis=-1, keepdims=True)
        y = xc * lax.rsqrt(var + LN_EPS) * lg_ref[...] + lb_ref[...]
        o_ref[0, pl.ds(r0, CONV_ROWS), :] = (y * _sigmoid(y)).astype(jnp.bfloat16)
        return carry

    lax.fori_loop(0, L // CONV_ROWS, tile, 0)


def _conv_branch(hglu, conv_w, conv_b, ln_g, ln_b):
    B, L, _ = hglu.shape
    const = lambda b: (0, 0)
    return pl.pallas_call(
        _conv_kernel,
        grid=(B,),
        in_specs=[
            pl.BlockSpec((1, L, C_CONV), lambda b: (b, 0, 0)),
            pl.BlockSpec(conv_w.shape, const),
            pl.BlockSpec(conv_b.shape, const),
            pl.BlockSpec(ln_g.shape, const),
            pl.BlockSpec(ln_b.shape, const),
        ],
        out_specs=pl.BlockSpec((1, L, C_CONV), lambda b: (b, 0, 0)),
        out_shape=jax.ShapeDtypeStruct((B, L, C_CONV), jnp.bfloat16),
        scratch_shapes=[pltpu.VMEM((CONV_PAD + L, C_CONV), jnp.float32)],
        compiler_params=pltpu.CompilerParams(
            dimension_semantics=("parallel",), vmem_limit_bytes=VMEM_LIMIT),
        name="conv_branch",
    )(hglu, conv_w, conv_b, ln_g, ln_b)


def _dsa_kernel(qit_ref, wit_ref, ki_ref, qt_ref, k_ref, vt_ref, o_ref,
                score_ref, bias_ref, acc_ref, qa_ref, m_ref, a_ref, s_ref, p_ref,
                *, topk):
    j = pl.program_id(1)
    nck = j + 1
    t_q = j * TQ + lax.broadcasted_iota(jnp.int32, (CK, TQ), 1)
    row_iota = lax.broadcasted_iota(jnp.int32, (CK, TQ), 0)

    def score_chunk(c, carry):
        kic = ki_ref[0, pl.ds(pl.multiple_of(c * CK, CK), CK), :]
        score = jnp.zeros((CK, TQ), jnp.float32)
        for h in range(H_IDX):
            rel = _dot(kic, qit_ref[0, h * D_IDX:(h + 1) * D_IDX, :])
            score = score + wit_ref[0, h:h + 1, :] * jnp.maximum(rel, 0.0)
        causal = (c * CK + row_iota) <= t_q
        score_ref[c] = jnp.where(causal, score, -jnp.inf)
        return carry

    lax.fori_loop(0, nck, score_chunk, 0)

    def sublane_allreduce(x, op):
        for sh in (4, 2, 1):
            x = op(x, pltpu.roll(x, sh, axis=0))
        return x

    def count(pred_fn):
        def body(c, acc):
            m = pred_fn(score_ref[c].reshape(CK // 8, 8, TQ)).astype(jnp.int32)
            return acc + jnp.sum(m.reshape(CK // 64, 8, 8, TQ), axis=0)
        acc = lax.fori_loop(0, nck, body, jnp.zeros((8, 8, TQ), jnp.int32))
        return sublane_allreduce(jnp.sum(acc, axis=0), jnp.add)

    def key_to_float(key):
        return pltpu.bitcast(jnp.where(key < 0, key ^ 0x7FFFFFFF, key), jnp.float32)

    def bit_step(i, u):
        cand_u = u | jnp.left_shift(jnp.int32(1), 31 - i)
        cand = key_to_float(cand_u ^ INT_MIN)
        cnt = count(lambda sc: sc >= cand[None])
        return jnp.where(cnt >= topk, cand_u, u)

    u = lax.fori_loop(0, 32, bit_step, jnp.zeros((8, TQ), jnp.int32))
    few = u == 0
    thr = jnp.where(few, -jnp.inf, key_to_float(u ^ INT_MIN))
    cnt_ge = count(lambda sc: sc >= thr[None])
    has_tie = jnp.logical_and(cnt_ge > topk, jnp.logical_not(few))
    any_tie = jnp.max(has_tie.astype(jnp.int32)) > 0

    def select_plain():
        thr_fin = jnp.where(few, jnp.finfo(jnp.float32).min, thr)[None]

        def body(c, carry):
            sc = score_ref[c].reshape(CK // 8, 8, TQ)
            bias_ref[c] = jnp.where(sc >= thr_fin, 0.0, NEG_BIG).reshape(CK, TQ)
            return carry

        lax.fori_loop(0, nck, body, 0)

    def select_ties(n):
        def fn():
            cnt_gt = count(lambda sc: sc > thr[None])
            need = jnp.where(few, 0, topk - cnt_gt).astype(jnp.float32)[None]
            tri = (lax.broadcasted_iota(jnp.int32, (CK, CK), 1)
                   <= lax.broadcasted_iota(jnp.int32, (CK, CK), 0)).astype(jnp.bfloat16)
            before = jnp.zeros((1, TQ), jnp.float32)
            for c in range(n):
                sc = score_ref[c].reshape(CK // 8, 8, TQ)
                eq = sc == thr[None]
                eq01 = jnp.where(eq, 1.0, 0.0).reshape(CK, TQ).astype(jnp.bfloat16)
                local = _dot(tri, eq01)
                rank = (local + before).reshape(CK // 8, 8, TQ)
                eq_rank = jnp.where(eq, rank, 3e38)
                bias = jnp.where(sc > thr[None], 0.0,
                                 jnp.where(eq_rank <= need, 0.0, NEG_BIG))
                bias_ref[c] = bias.reshape(CK, TQ)
                before = before + local[CK - 1:CK, :]
        return fn

    def select_with_ties():
        lax.switch(j, [select_ties(n + 1) for n in range(score_ref.shape[0])])

    lax.cond(any_tie, select_with_ties, select_plain)

    frow = lax.broadcasted_iota(jnp.int32, (KAUG - HEAD_DIM, TQ), 0)
    for h in range(N_HEADS):
        slope = _alibi_slope(h)
        feat = jnp.where(frow == 0, slope * POS_SPLIT, jnp.where(frow == 1, slope, 0.0))
        qa_ref[h, :HEAD_DIM, :] = qt_ref[0, h * HEAD_DIM:(h + 1) * HEAD_DIM, :]
        qa_ref[h, HEAD_DIM:, :] = feat.astype(jnp.bfloat16)
    m_ref[...] = jnp.full(m_ref.shape, NEG_BIG, jnp.float32)
    acc_ref[...] = jnp.zeros(acc_ref.shape, jnp.float32)

    def att_chunk(c, carry):
        kc = k_ref[0, pl.ds(pl.multiple_of(c * CK, CK), CK), :]
        bias = bias_ref[c]
        vtc = vt_ref[0, c]
        for h in range(N_HEADS):
            s = _dot(kc, qa_ref[h]) + bias
            s_ref[h] = s
            mx = jnp.max(s.reshape(CK // 8, 8, TQ), axis=0)
            m_old = m_ref[h]
            m_new = jnp.maximum(m_old, sublane_allreduce(mx, jnp.maximum))
            a_ref[h] = jnp.exp(m_old - m_new)
            m_ref[h] = m_new
        for h in range(N_HEADS):
            p = jnp.exp(s_ref[h].reshape(CK // 8, 8, TQ) - m_ref[h][None])
            p_ref[h] = p.reshape(CK, TQ).astype(jnp.bfloat16)
        for h in range(N_HEADS):
            pv = _dot(vtc, p_ref[h])
            acc = acc_ref[h].reshape(VAUG // 8, 8, TQ)
            acc_ref[h] = (a_ref[h][None] * acc).reshape(VAUG, TQ) + pv
        return carry

    lax.fori_loop(0, nck, att_chunk, 0)

    hps = CK // HEAD_DIM
    for h in range(N_HEADS):
        acc = acc_ref[h, :HEAD_DIM, :].reshape(HEAD_DIM // 8, 8, TQ)
        denom = acc_ref[h, HEAD_DIM:HEAD_DIM + 8, :]
        r0 = (h % hps) * HEAD_DIM
        s_ref[h // hps, r0:r0 + HEAD_DIM, :] = (acc / denom[None]).reshape(HEAD_DIM, TQ)
    for g in range(N_HEADS // hps):
        o_ref[0, :, g * CK:(g + 1) * CK] = s_ref[g].T.astype(jnp.bfloat16)


def _dsa(qit, wit, ki, qt, k, vt):
    B, _, L = qt.shape
    nq = L // TQ
    nck = L // CK
    topk = min(TOPK_MAX, L // 4)
    per_b = lambda b, j: (b, 0, 0)
    col = lambda b, j: (b, 0, j)
    return pl.pallas_call(
        functools.partial(_dsa_kernel, topk=topk),
        grid=(B, nq),
        in_specs=[
            pl.BlockSpec((1, H_IDX * D_IDX, TQ), col),
            pl.BlockSpec((1, H_IDX, TQ), col),
            pl.BlockSpec((1, L, D_IDX), per_b),
            pl.BlockSpec((1, N_HEADS * HEAD_DIM, TQ), col),
            pl.BlockSpec((1, L, KAUG), per_b),
            pl.BlockSpec((1, nck, VAUG, CK), lambda b, j: (b, 0, 0, 0)),
        ],
        out_specs=pl.BlockSpec((1, TQ, N_HEADS * HEAD_DIM), lambda b, j: (b, j, 0)),
        out_shape=jax.ShapeDtypeStruct((B, L, N_HEADS * HEAD_DIM), jnp.bfloat16),
        scratch_shapes=[
            pltpu.VMEM((nck, CK, TQ), jnp.float32),
            pltpu.VMEM((nck, CK, TQ), jnp.float32),
            pltpu.VMEM((N_HEADS, VAUG, TQ), jnp.float32),
            pltpu.VMEM((N_HEADS, KAUG, TQ), jnp.bfloat16),
            pltpu.VMEM((N_HEADS, 8, TQ), jnp.float32),
            pltpu.VMEM((N_HEADS, 8, TQ), jnp.float32),
            pltpu.VMEM((N_HEADS, CK, TQ), jnp.float32),
            pltpu.VMEM((N_HEADS, CK, TQ), jnp.bfloat16),
        ],
        compiler_params=pltpu.CompilerParams(
            dimension_semantics=("parallel", "parallel"), vmem_limit_bytes=VMEM_LIMIT),
        name="dsa_attention",
    )(qit, wit, ki, qt, k, vt)


def _post_kernel(x_ref, hc_ref, at_ref, g_ref, wco_ref, wao_ref, wo_ref, nf_ref,
                 wfg_ref, wfu_ref, wfd_ref, nl_ref, o_ref, *, final_norm, ff_chunks):
    D = x_ref.shape[-1]
    y_conv = _dot(hc_ref[...], wco_ref[...])
    y_attn = _dot(at_ref[...], wao_ref[...])
    g = g_ref[...].astype(jnp.float32)
    merged = g[:, :D] * y_conv + g[:, D:] * y_attn
    x1 = x_ref[...] + _dot(merged.astype(jnp.bfloat16), wo_ref[...])

    ms = jnp.mean(x1 * x1, axis=-1, keepdims=True)
    hf = (x1 * lax.rsqrt(ms + RMS_EPS) * nf_ref[...]).astype(jnp.bfloat16)
    dff = wfg_ref.shape[1]
    fc = dff // ff_chunks
    x2 = x1
    for c in range(ff_chunks):
        gt = _dot(hf, wfg_ref[:, c * fc:(c + 1) * fc])
        up = _dot(hf, wfu_ref[:, c * fc:(c + 1) * fc])
        act = (gt * _sigmoid(gt) * up).astype(jnp.bfloat16)
        x2 = x2 + _dot(act, wfd_ref[c * fc:(c + 1) * fc, :])
    if final_norm:
        ms2 = jnp.mean(x2 * x2, axis=-1, keepdims=True)
        x2 = x2 * lax.rsqrt(ms2 + RMS_EPS) * nl_ref[...]
    o_ref[...] = x2


def _post(x2d, hc, at, g, wco, wao, wo, nf, wfg, wfu, wfd, nl, final_norm):
    T, D = x2d.shape
    const = lambda i: (0, 0)
    row = lambda i: (i, 0)
    single = pl.Buffered(1)

    def wspec(a):
        return pl.BlockSpec(a.shape, const, pipeline_mode=single)

    return pl.pallas_call(
        functools.partial(_post_kernel, final_norm=final_norm, ff_chunks=2),
        grid=(T // TM_POST,),
        in_specs=[
            pl.BlockSpec((TM_POST, D), row),
            pl.BlockSpec((TM_POST, hc.shape[1]), row),
            pl.BlockSpec((TM_POST, at.shape[1]), row),
            pl.BlockSpec((TM_POST, g.shape[1]), row),
            wspec(wco), wspec(wao), wspec(wo), wspec(nf),
            wspec(wfg), wspec(wfu), wspec(wfd), wspec(nl),
        ],
        out_specs=pl.BlockSpec((TM_POST, D), row),
        out_shape=jax.ShapeDtypeStruct((T, D), jnp.float32),
        compiler_params=pltpu.CompilerParams(
            dimension_semantics=("parallel",), vmem_limit_bytes=VMEM_LIMIT),
        name="post",
    )(x2d, hc, at, g, wco, wao, wo, nf, wfg, wfu, wfd, nl)


def kernel(x, norm_mix, w_in, b_gate, conv_w, conv_b, conv_ln_g, conv_ln_b, w_conv_out,
           w_attn_out, w_out, norm_ffn, w_ffn_gate, w_ffn_up, w_ffn_down, norm_final):
    B, L, D = x.shape
    depth = w_in.shape[0]
    bf = jnp.bfloat16
    nq = N_HEADS * HEAD_DIM
    nqi = H_IDX * D_IDX
    o_q = 2 * C_CONV
    o_k = o_q + nq
    o_v = o_k + HEAD_DIM
    o_qi = o_v + HEAD_DIM
    o_ki = o_qi + nqi
    o_wi = o_ki + D_IDX
    o_g = o_wi + H_IDX

    for layer in range(depth):
        w = w_in[layer]
        wc = w[:, :o_q].astype(bf)
        wg = w[:, o_g:].astype(bf)
        wkk = jnp.concatenate([w[:, o_k:o_v], w[:, o_ki:o_wi]], axis=1).astype(bf)
        wt = jnp.concatenate([
            w[:, o_q:o_k] * (HEAD_DIM ** -0.5),
            w[:, o_qi:o_ki] * (D_IDX ** -0.5),
            w[:, o_v:o_qi],
            w[:, o_wi:o_g],
            jnp.zeros((D, 128 - HEAD_DIM - H_IDX), w.dtype),
        ], axis=1).T.astype(bf)

        hglu, gate, k, ki, qt, qit, vt, wit = _inproj(
            x, norm_mix[layer][None, :], wc, wg, b_gate[layer][None, :], wkk, wt)
        hc = _conv_branch(hglu, conv_w[layer], conv_b[layer][None, :],
                          conv_ln_g[layer][None, :], conv_ln_b[layer][None, :])
        at = _dsa(qit, wit, ki, qt, k, vt)
        x = _post(
            x.reshape(B * L, D), hc.reshape(B * L, C_CONV), at.reshape(B * L, nq),
            gate.reshape(B * L, 2 * D),
            w_conv_out[layer].astype(bf), w_attn_out[layer].astype(bf), w_out[layer].astype(bf),
            norm_ffn[layer][None, :], w_ffn_gate[layer].astype(bf), w_ffn_up[layer].astype(bf),
            w_ffn_down[layer].astype(bf), norm_final[None, :],
            final_norm=(layer == depth - 1)).reshape(B, L, D)
    return x
```

```python
import functools

import jax
import jax.numpy as jnp
from jax import lax
from jax.experimental import pallas as pl
from jax.experimental.pallas import tpu as pltpu

C_CONV = 512
CONV_WIDTH = 31
N_HEADS = 8
HEAD_DIM = 64
H_IDX = 8
D_IDX = 64
TOPK_MAX = 256
RMS_EPS = 1e-6
LN_EPS = 1e-5

V7X_VMEM_BYTES = 64 * 1024 * 1024
VMEM_LIMIT = 56 * 1024 * 1024

TM_IN = 512
TM_POST = 256
TQ = 256
CK = 256
CONV_ROWS = 256
CONV_PAD = 32
KAUG = 128
POS_SPLIT = 64
VAUG = 80
NEG_BIG = -1e30
INT_MIN = -2 ** 31

_NT = (((1,), (1,)), ((), ()))


def _alibi_slope(h):
    return 2.0 ** (-8.0 * (h + 1) / N_HEADS)


def _dot(a, b):
    return jnp.dot(a, b, preferred_element_type=jnp.float32)


def _sigmoid(x):
    return 1.0 / (1.0 + jnp.exp(-x))


def _inproj_kernel(x_ref, gn_ref, wc_ref, wg_ref, bg_ref, wkk_ref, wt_ref,
                   hglu_ref, gate_ref, k_ref, ki_ref, qt_ref, qit_ref, vt_ref, wit_ref):
    x = x_ref[0]
    ms = jnp.mean(x * x, axis=-1, keepdims=True)
    h = (x * lax.rsqrt(ms + RMS_EPS) * gn_ref[...]).astype(jnp.bfloat16)

    uc = _dot(h, wc_ref[...])
    hglu_ref[0] = uc[:, :C_CONV] * _sigmoid(uc[:, C_CONV:])

    ug = _dot(h, wg_ref[...]) + bg_ref[...]
    gate_ref[0] = _sigmoid(ug).astype(jnp.bfloat16)

    ukk = _dot(h, wkk_ref[...])
    lane = lax.broadcasted_iota(jnp.int32, (TM_IN, KAUG), 1)
    pos = pl.program_id(1) * TM_IN + lax.broadcasted_iota(jnp.int32, (TM_IN, KAUG), 0)
    feat = jnp.where(lane == HEAD_DIM, pos // POS_SPLIT,
                     jnp.where(lane == HEAD_DIM + 1, pos % POS_SPLIT, 0)).astype(jnp.float32)
    k_ref[0] = jnp.where(lane < HEAD_DIM, ukk, feat).astype(jnp.bfloat16)
    ki_ref[0] = ukk[:, HEAD_DIM:].astype(jnp.bfloat16)

    ut = lax.dot_general(wt_ref[...], h, _NT, preferred_element_type=jnp.float32)
    nq = N_HEADS * HEAD_DIM
    nqi = H_IDX * D_IDX
    qt_ref[0] = ut[:nq].astype(jnp.bfloat16)
    qit_ref[0] = ut[nq:nq + nqi].astype(jnp.bfloat16)
    vt = ut[nq + nqi:nq + nqi + HEAD_DIM].astype(jnp.bfloat16)
    ones_rows = (lax.broadcasted_iota(jnp.int32, (VAUG - HEAD_DIM, CK), 0) < 8).astype(jnp.bfloat16)
    for c in range(TM_IN // CK):
        vt_ref[0, c, :HEAD_DIM, :] = vt[:, c * CK:(c + 1) * CK]
        vt_ref[0, c, HEAD_DIM:, :] = ones_rows
    o = nq + nqi + HEAD_DIM
    wit_ref[0] = ut[o:o + H_IDX] * (H_IDX ** -0.5)


def _inproj(x, gn, wc, wg, bg, wkk, wt):
    B, L, D = x.shape
    nt = L // TM_IN
    const = lambda b, i: (0, 0)
    row = lambda b, i: (b, i, 0)
    col = lambda b, i: (b, 0, i)
    bf = jnp.bfloat16
    out_shape = (
        jax.ShapeDtypeStruct((B, L, C_CONV), jnp.float32),
        jax.ShapeDtypeStruct((B, L, 2 * D), bf),
        jax.ShapeDtypeStruct((B, L, KAUG), bf),
        jax.ShapeDtypeStruct((B, L, D_IDX), bf),
        jax.ShapeDtypeStruct((B, N_HEADS * HEAD_DIM, L), bf),
        jax.ShapeDtypeStruct((B, H_IDX * D_IDX, L), bf),
        jax.ShapeDtypeStruct((B, L // CK, VAUG, CK), bf),
        jax.ShapeDtypeStruct((B, H_IDX, L), jnp.float32),
    )
    out_specs = (
        pl.BlockSpec((1, TM_IN, C_CONV), row),
        pl.BlockSpec((1, TM_IN, 2 * D), row),
        pl.BlockSpec((1, TM_IN, KAUG), row),
        pl.BlockSpec((1, TM_IN, D_IDX), row),
        pl.BlockSpec((1, N_HEADS * HEAD_DIM, TM_IN), col),
        pl.BlockSpec((1, H_IDX * D_IDX, TM_IN), col),
        pl.BlockSpec((1, TM_IN // CK, VAUG, CK), lambda b, i: (b, i, 0, 0)),
        pl.BlockSpec((1, H_IDX, TM_IN), col),
    )
    in_specs = [
        pl.BlockSpec((1, TM_IN, D), row),
        pl.BlockSpec(gn.shape, const),
        pl.BlockSpec(wc.shape, const),
        pl.BlockSpec(wg.shape, const),
        pl.BlockSpec(bg.shape, const),
        pl.BlockSpec(wkk.shape, const),
        pl.BlockSpec(wt.shape, const),
    ]
    return pl.pallas_call(
        _inproj_kernel,
        grid=(B, nt),
        in_specs=in_specs,
        out_specs=out_specs,
        out_shape=out_shape,
        compiler_params=pltpu.CompilerParams(
            dimension_semantics=("parallel", "parallel"), vmem_limit_bytes=VMEM_LIMIT),
        name="inproj",
    )(x, gn, wc, wg, bg, wkk, wt)


def _conv_kernel(h_ref, w_ref, b_ref, lg_ref, lb_ref, o_ref, pad_ref):
    L = h_ref.shape[1]
    pad_ref[0:CONV_PAD, :] = jnp.zeros((CONV_PAD, C_CONV), jnp.float32)
    pad_ref[CONV_PAD:, :] = h_ref[0]
    shift = CONV_PAD - (CONV_WIDTH - 1)

    def tile(t, carry):
        r0 = pl.multiple_of(t * CONV_ROWS, CONV_ROWS)
        acc = jnp.zeros((CONV_ROWS, C_CONV), jnp.float32) + b_ref[...]
        win = pad_ref[pl.ds(r0, CONV_ROWS + CONV_PAD), :]
        for r in range(8):
            rolled = win if r == 0 else pltpu.roll(win, CONV_ROWS + CONV_PAD - r, axis=0)
            for k in range(CONV_WIDTH):
                if (shift + k) % 8 == r:
                    off = shift + k - r
                    acc = acc + w_ref[k:k + 1, :] * rolled[off:off + CONV_ROWS]
        mu = jnp.mean(acc, axis=-1, keepdims=True)
        xc = acc - mu
        var = jnp.mean(xc * xc, axis=-1, keepdims=True)
        y = xc * lax.rsqrt(var + LN_EPS) * lg_ref[...] + lb_ref[...]
        o_ref[0, pl.ds(r0, CONV_ROWS), :] = (y * _sigmoid(y)).astype(jnp.bfloat16)
        return carry

    lax.fori_loop(0, L // CONV_ROWS, tile, 0)


def _conv_branch(hglu, conv_w, conv_b, ln_g, ln_b):
    B, L, _ = hglu.shape
    const = lambda b: (0, 0)
    return pl.pallas_call(
        _conv_kernel,
        grid=(B,),
        in_specs=[
            pl.BlockSpec((1, L, C_CONV), lambda b: (b, 0, 0)),
            pl.BlockSpec(conv_w.shape, const),
            pl.BlockSpec(conv_b.shape, const),
            pl.BlockSpec(ln_g.shape, const),
            pl.BlockSpec(ln_b.shape, const),
        ],
        out_specs=pl.BlockSpec((1, L, C_CONV), lambda b: (b, 0, 0)),
        out_shape=jax.ShapeDtypeStruct((B, L, C_CONV), jnp.bfloat16),
        scratch_shapes=[pltpu.VMEM((CONV_PAD + L, C_CONV), jnp.float32)],
        compiler_params=pltpu.CompilerParams(
            dimension_semantics=("parallel",), vmem_limit_bytes=VMEM_LIMIT),
        name="conv_branch",
    )(hglu, conv_w, conv_b, ln_g, ln_b)


def _dsa_kernel(qit_ref, wit_ref, ki_ref, qt_ref, k_ref, vt_ref, o_ref,
                score_ref, sb_ref, bias_ref, acc_ref, qa_ref, m_ref, a_ref, s_ref, p_ref,
                *, topk):
    j = pl.program_id(1)
    nck = j + 1
    t_q = j * TQ + lax.broadcasted_iota(jnp.int32, (CK, TQ), 1)
    row_iota = lax.broadcasted_iota(jnp.int32, (CK, TQ), 0)

    def score_chunk(c, carry):
        kic = ki_ref[0, pl.ds(pl.multiple_of(c * CK, CK), CK), :]
        score = jnp.zeros((CK, TQ), jnp.float32)
        for h in range(H_IDX):
            rel = _dot(kic, qit_ref[0, h * D_IDX:(h + 1) * D_IDX, :])
            score = score + wit_ref[0, h:h + 1, :] * jnp.maximum(rel, 0.0)
        causal = (c * CK + row_iota) <= t_q
        score = jnp.where(causal, score, -jnp.inf)
        score_ref[c] = score
        sb_ref[c] = score.astype(jnp.bfloat16)
        return carry

    lax.fori_loop(0, nck, score_chunk, 0)

    def sublane_allreduce(x, op):
        for sh in (4, 2, 1):
            x = op(x, pltpu.roll(x, sh, axis=0))
        return x

    def count(pred_fn):
        def body(c, acc):
            m = pred_fn(score_ref[c].reshape(CK // 8, 8, TQ)).astype(jnp.int32)
            return acc + jnp.sum(m.reshape(CK // 64, 8, 8, TQ), axis=0)
        acc = lax.fori_loop(0, nck, body, jnp.zeros((8, 8, TQ), jnp.int32))
        return sublane_allreduce(jnp.sum(acc, axis=0), jnp.add)

    def key_to_float(key):
        return pltpu.bitcast(jnp.where(key < 0, key ^ 0x7FFFFFFF, key), jnp.float32)

    def count16(cand):
        cand_b = jnp.concatenate([cand, cand], axis=0).astype(jnp.bfloat16)[None]
        one = jnp.ones((), jnp.bfloat16)
        zero = jnp.zeros((), jnp.bfloat16)

        def body(c, acc):
            sb = sb_ref[c].reshape(CK // 16, 16, TQ)
            parts = [jnp.where(sb[r:r + 1] >= cand_b, one, zero) for r in range(CK // 16)]
            while len(parts) > 1:
                parts = [parts[r] + parts[r + 1] for r in range(0, len(parts), 2)]
            return acc + parts[0][0].astype(jnp.float32)

        acc = lax.fori_loop(0, nck, body, jnp.zeros((16, TQ), jnp.float32))
        return sublane_allreduce(acc[:8] + acc[8:], jnp.add)

    def key16_to_float(k16):
        bits16 = jnp.where(k16 < 0, k16 ^ 0x7FFF, k16) & 0xFFFF
        return pltpu.bitcast(jnp.left_shift(bits16, 16), jnp.float32)

    def bit_step16(i, u):
        cand_u = u | jnp.left_shift(jnp.int32(1), 15 - i)
        cnt = count16(key16_to_float(cand_u - 32768))
        return jnp.where(cnt >= topk, cand_u, u)

    u16 = lax.fori_loop(0, 16, bit_step16, jnp.zeros((8, TQ), jnp.int32))
    few = u16 == 0
    t16_bits = pltpu.bitcast(key16_to_float(u16 - 32768), jnp.int32)
    base = jnp.where(t16_bits < 0, t16_bits ^ 0x7FFFFFFF, t16_bits) - 65536

    def bit_step(i, u):
        cand_u = u | jnp.left_shift(jnp.int32(1), 16 - i)
        cand = key_to_float(base + cand_u)
        cnt = count(lambda sc: sc >= cand[None])
        return jnp.where(cnt >= topk, cand_u, u)

    u = lax.fori_loop(0, 17, bit_step, jnp.zeros((8, TQ), jnp.int32))
    thr = jnp.where(few, -jnp.inf, key_to_float(base + u))
    cnt_ge = count(lambda sc: sc >= thr[None])
    has_tie = jnp.logical_and(cnt_ge > topk, jnp.logical_not(few))
    any_tie = jnp.max(has_tie.astype(jnp.int32)) > 0

    def select_plain():
        thr_fin = jnp.where(few, jnp.finfo(jnp.float32).min, thr)[None]

        def body(c, carry):
            sc = score_ref[c].reshape(CK // 8, 8, TQ)
            bias_ref[c] = jnp.where(sc >= thr_fin, 0.0, NEG_BIG).reshape(CK, TQ)
            return carry

        lax.fori_loop(0, nck, body, 0)

    def select_ties(n):
        def fn():
            cnt_gt = count(lambda sc: sc > thr[None])
            need = jnp.where(few, 0, topk - cnt_gt).astype(jnp.float32)[None]
            tri = (lax.broadcasted_iota(jnp.int32, (CK, CK), 1)
                   <= lax.broadcasted_iota(jnp.int32, (CK, CK), 0)).astype(jnp.bfloat16)
            before = jnp.zeros((1, TQ), jnp.float32)
            for c in range(n):
                sc = score_ref[c].reshape(CK // 8, 8, TQ)
                eq = sc == thr[None]
                eq01 = jnp.where(eq, 1.0, 0.0).reshape(CK, TQ).astype(jnp.bfloat16)
                local = _dot(tri, eq01)
                rank = (local + before).reshape(CK // 8, 8, TQ)
                eq_rank = jnp.where(eq, rank, 3e38)
                bias = jnp.where(sc > thr[None], 0.0,
                                 jnp.where(eq_rank <= need, 0.0, NEG_BIG))
                bias_ref[c] = bias.reshape(CK, TQ)
                before = before + local[CK - 1:CK, :]
        return fn

    def select_with_ties():
        lax.switch(j, [select_ties(n + 1) for n in range(score_ref.shape[0])])

    lax.cond(any_tie, select_with_ties, select_plain)

    frow = lax.broadcasted_iota(jnp.int32, (KAUG - HEAD_DIM, TQ), 0)
    for h in range(N_HEADS):
        slope = _alibi_slope(h)
        feat = jnp.where(frow == 0, slope * POS_SPLIT, jnp.where(frow == 1, slope, 0.0))
        qa_ref[h, :HEAD_DIM, :] = qt_ref[0, h * HEAD_DIM:(h + 1) * HEAD_DIM, :]
        qa_ref[h, HEAD_DIM:, :] = feat.astype(jnp.bfloat16)
    m_ref[...] = jnp.full(m_ref.shape, NEG_BIG, jnp.float32)
    acc_ref[...] = jnp.zeros(acc_ref.shape, jnp.float32)

    def att_chunk(c, carry):
        kc = k_ref[0, pl.ds(pl.multiple_of(c * CK, CK), CK), :]
        bias = bias_ref[c]
        vtc = vt_ref[0, c]
        for h in range(N_HEADS):
            s = _dot(kc, qa_ref[h]) + bias
            s_ref[h] = s
            mx = jnp.max(s.reshape(CK // 8, 8, TQ), axis=0)
            m_old = m_ref[h]
            m_new = jnp.maximum(m_old, sublane_allreduce(mx, jnp.maximum))
            a_ref[h] = jnp.exp(m_old - m_new)
            m_ref[h] = m_new
        for h in range(N_HEADS):
            p = jnp.exp(s_ref[h].reshape(CK // 8, 8, TQ) - m_ref[h][None])
            p_ref[h] = p.reshape(CK, TQ).astype(jnp.bfloat16)
        for h in range(N_HEADS):
            pv = _dot(vtc, p_ref[h])
            acc = acc_ref[h].reshape(VAUG // 8, 8, TQ)
            acc_ref[h] = (a_ref[h][None] * acc).reshape(VAUG, TQ) + pv
        return carry

    lax.fori_loop(0, nck, att_chunk, 0)

    hps = CK // HEAD_DIM
    for h in range(N_HEADS):
        acc = acc_ref[h, :HEAD_DIM, :].reshape(HEAD_DIM // 8, 8, TQ)
        denom = acc_ref[h, HEAD_DIM:HEAD_DIM + 8, :]
        r0 = (h % hps) * HEAD_DIM
        s_ref[h // hps, r0:r0 + HEAD_DIM, :] = (acc / denom[None]).reshape(HEAD_DIM, TQ)
    for g in range(N_HEADS // hps):
        o_ref[0, :, g * CK:(g + 1) * CK] = s_ref[g].T.astype(jnp.bfloat16)


def _dsa(qit, wit, ki, qt, k, vt):
    B, _, L = qt.shape
    nq = L // TQ
    nck = L // CK
    topk = min(TOPK_MAX, L // 4)
    per_b = lambda b, j: (b, 0, 0)
    col = lambda b, j: (b, 0, j)
    return pl.pallas_call(
        functools.partial(_dsa_kernel, topk=topk),
        grid=(B, nq),
        in_specs=[
            pl.BlockSpec((1, H_IDX * D_IDX, TQ), col),
            pl.BlockSpec((1, H_IDX, TQ), col),
            pl.BlockSpec((1, L, D_IDX), per_b),
            pl.BlockSpec((1, N_HEADS * HEAD_DIM, TQ), col),
            pl.BlockSpec((1, L, KAUG), per_b),
            pl.BlockSpec((1, nck, VAUG, CK), lambda b, j: (b, 0, 0, 0)),
        ],
        out_specs=pl.BlockSpec((1, TQ, N_HEADS * HEAD_DIM), lambda b, j: (b, j, 0)),
        out_shape=jax.ShapeDtypeStruct((B, L, N_HEADS * HEAD_DIM), jnp.bfloat16),
        scratch_shapes=[
            pltpu.VMEM((nck, CK, TQ), jnp.float32),
            pltpu.VMEM((nck, CK, TQ), jnp.bfloat16),
            pltpu.VMEM((nck, CK, TQ), jnp.float32),
            pltpu.VMEM((N_HEADS, VAUG, TQ), jnp.float32),
            pltpu.VMEM((N_HEADS, KAUG, TQ), jnp.bfloat16),
            pltpu.VMEM((N_HEADS, 8, TQ), jnp.float32),
            pltpu.VMEM((N_HEADS, 8, TQ), jnp.float32),
            pltpu.VMEM((N_HEADS, CK, TQ), jnp.float32),
            pltpu.VMEM((N_HEADS, CK, TQ), jnp.bfloat16),
        ],
        compiler_params=pltpu.CompilerParams(
            dimension_semantics=("parallel", "parallel"), vmem_limit_bytes=VMEM_LIMIT),
        name="dsa_attention",
    )(qit, wit, ki, qt, k, vt)


def _post_kernel(x_ref, hc_ref, at_ref, g_ref, wco_ref, wao_ref, wo_ref, nf_ref,
                 wfg_ref, wfu_ref, wfd_ref, nl_ref, o_ref, *, final_norm, ff_chunks):
    D = x_ref.shape[-1]
    y_conv = _dot(hc_ref[...], wco_ref[...])
    y_attn = _dot(at_ref[...], wao_ref[...])
    g = g_ref[...].astype(jnp.float32)
    merged = g[:, :D] * y_conv + g[:, D:] * y_attn
    x1 = x_ref[...] + _dot(merged.astype(jnp.bfloat16), wo_ref[...])

    ms = jnp.mean(x1 * x1, axis=-1, keepdims=True)
    hf = (x1 * lax.rsqrt(ms + RMS_EPS) * nf_ref[...]).astype(jnp.bfloat16)
    dff = wfg_ref.shape[1]
    fc = dff // ff_chunks
    x2 = x1
    for c in range(ff_chunks):
        gt = _dot(hf, wfg_ref[:, c * fc:(c + 1) * fc])
        up = _dot(hf, wfu_ref[:, c * fc:(c + 1) * fc])
        act = (gt * _sigmoid(gt) * up).astype(jnp.bfloat16)
        x2 = x2 + _dot(act, wfd_ref[c * fc:(c + 1) * fc, :])
    if final_norm:
        ms2 = jnp.mean(x2 * x2, axis=-1, keepdims=True)
        x2 = x2 * lax.rsqrt(ms2 + RMS_EPS) * nl_ref[...]
    o_ref[...] = x2


def _post(x2d, hc, at, g, wco, wao, wo, nf, wfg, wfu, wfd, nl, final_norm):
    T, D = x2d.shape
    const = lambda i: (0, 0)
    row = lambda i: (i, 0)
    single = pl.Buffered(1)

    def wspec(a):
        return pl.BlockSpec(a.shape, const, pipeline_mode=single)

    return pl.pallas_call(
        functools.partial(_post_kernel, final_norm=final_norm, ff_chunks=2),
        grid=(T // TM_POST,),
        in_specs=[
            pl.BlockSpec((TM_POST, D), row),
            pl.BlockSpec((TM_POST, hc.shape[1]), row),
            pl.BlockSpec((TM_POST, at.shape[1]), row),
            pl.BlockSpec((TM_POST, g.shape[1]), row),
            wspec(wco), wspec(wao), wspec(wo), wspec(nf),
            wspec(wfg), wspec(wfu), wspec(wfd), wspec(nl),
        ],
        out_specs=pl.BlockSpec((TM_POST, D), row),
        out_shape=jax.ShapeDtypeStruct((T, D), jnp.float32),
        compiler_params=pltpu.CompilerParams(
            dimension_semantics=("parallel",), vmem_limit_bytes=VMEM_LIMIT),
        name="post",
    )(x2d, hc, at, g, wco, wao, wo, nf, wfg, wfu, wfd, nl)


def kernel(x, norm_mix, w_in, b_gate, conv_w, conv_b, conv_ln_g, conv_ln_b, w_conv_out,
           w_attn_out, w_out, norm_ffn, w_ffn_gate, w_ffn_up, w_ffn_down, norm_final):
    B, L, D = x.shape
    depth = w_in.shape[0]
    bf = jnp.bfloat16
    nq = N_HEADS * HEAD_DIM
    nqi = H_IDX * D_IDX
    o_q = 2 * C_CONV
    o_k = o_q + nq
    o_v = o_k + HEAD_DIM
    o_qi = o_v + HEAD_DIM
    o_ki = o_qi + nqi
    o_wi = o_ki + D_IDX
    o_g = o_wi + H_IDX

    for layer in range(depth):
        w = w_in[layer]
        wc = w[:, :o_q].astype(bf)
        wg = w[:, o_g:].astype(bf)
        wkk = jnp.concatenate([w[:, o_k:o_v], w[:, o_ki:o_wi]], axis=1).astype(bf)
        wt = jnp.concatenate([
            w[:, o_q:o_k] * (HEAD_DIM ** -0.5),
            w[:, o_qi:o_ki] * (D_IDX ** -0.5),
            w[:, o_v:o_qi],
            w[:, o_wi:o_g],
            jnp.zeros((D, 128 - HEAD_DIM - H_IDX), w.dtype),
        ], axis=1).T.astype(bf)

        hglu, gate, k, ki, qt, qit, vt, wit = _inproj(
            x, norm_mix[layer][None, :], wc, wg, b_gate[layer][None, :], wkk, wt)
        hc = _conv_branch(hglu, conv_w[layer], conv_b[layer][None, :],
                          conv_ln_g[layer][None, :], conv_ln_b[layer][None, :])
        at = _dsa(qit, wit, ki, qt, k, vt)
        x = _post(
            x.reshape(B * L, D), hc.reshape(B * L, C_CONV), at.reshape(B * L, nq),
            gate.reshape(B * L, 2 * D),
            w_conv_out[layer].astype(bf), w_attn_out[layer].astype(bf), w_out[layer].astype(bf),
            norm_ffn[layer][None, :], w_ffn_gate[layer].astype(bf), w_ffn_up[layer].astype(bf),
            w_ffn_down[layer].astype(bf), norm_final[None, :],
            final_norm=(layer == depth - 1)).reshape(B, L, D)
    return x
```

```python
import functools

import jax
import jax.numpy as jnp
from jax import lax
from jax.experimental import pallas as pl
from jax.experimental.pallas import tpu as pltpu

C_CONV = 512
CONV_WIDTH = 31
N_HEADS = 8
HEAD_DIM = 64
H_IDX = 8
D_IDX = 64
TOPK_MAX = 256
RMS_EPS = 1e-6
LN_EPS = 1e-5

V7X_VMEM_BYTES = 64 * 1024 * 1024
VMEM_LIMIT = 56 * 1024 * 1024

TM_IN = 512
TM_POST = 512
TQ = 256
CK = 256
LANES = 128
N_STRIPS = C_CONV // LANES
CONV_ROWS = 128
CONV_PAD = 32
KAUG = 128
POS_SPLIT = 64
VAUG = 80
NEG_BIG = -1e30
INT_MIN = -2 ** 31

_NT = (((1,), (1,)), ((), ()))


def _alibi_slope(h):
    return 2.0 ** (-8.0 * (h + 1) / N_HEADS)


def _dot(a, b):
    return jnp.dot(a, b, preferred_element_type=jnp.float32)


def _sigmoid(x):
    return 1.0 / (1.0 + jnp.exp(-x))


def _conv_strip(win_ref, out_ref, s, w_ref, b_ref):
    shift = CONV_PAD - (CONV_WIDTH - 1)
    for t in range(TM_IN // CONV_ROWS):
        r0 = t * CONV_ROWS
        acc = jnp.zeros((CONV_ROWS, LANES), jnp.float32) + b_ref[s]
        for k in range(CONV_WIDTH):
            off = r0 + shift + k
            acc = acc + w_ref[s, k:k + 1, :] * win_ref[s, off:off + CONV_ROWS, :]
        out_ref[s, r0:r0 + CONV_ROWS, :] = acc


def _inproj_kernel(x_ref, gn_ref, wc_ref, wg_ref, bg_ref, wkk_ref, wt_ref,
                   cw_ref, cb_ref, lg_ref, lb_ref,
                   hc_ref, gate_ref, k_ref, ki_ref, qt_ref, qit_ref, vt_ref, wit_ref,
                   h_ref, win_ref, cout_ref, ut_ref):
    x = x_ref[0]
    ms = jnp.mean(x * x, axis=-1, keepdims=True)
    h = (x * lax.rsqrt(ms + RMS_EPS) * gn_ref[...]).astype(jnp.bfloat16)
    h_ref[...] = h

    uc = _dot(h, wc_ref[...])

    @pl.when(pl.program_id(1) == 0)
    def _():
        win_ref[:, 0:CONV_PAD, :] = jnp.zeros((N_STRIPS, CONV_PAD, LANES), jnp.float32)

    hglu = uc[:, :C_CONV] * _sigmoid(uc[:, C_CONV:])
    for c in range(N_STRIPS):
        win_ref[c, CONV_PAD:, :] = hglu[:, c * LANES:(c + 1) * LANES]

    def strip_step(s, carry):
        _conv_strip(win_ref, cout_ref, s, cw_ref, cb_ref)
        ug = _dot(h_ref[...], wg_ref[s]) + bg_ref[s]
        gate_ref[s, 0] = _sigmoid(ug).astype(jnp.bfloat16)
        ut_ref[s] = lax.dot_general(wt_ref[s], h_ref[...], _NT,
                                    preferred_element_type=jnp.float32)
        return carry

    lax.fori_loop(0, N_STRIPS, strip_step, 0)
    win_ref[:, 0:CONV_PAD, :] = win_ref[:, TM_IN:TM_IN + CONV_PAD, :]

    for t in range(TM_IN // CONV_ROWS):
        rows = slice(t * CONV_ROWS, (t + 1) * CONV_ROWS)
        conv = jnp.concatenate([cout_ref[c, rows, :] for c in range(N_STRIPS)], axis=1)
        mu = jnp.mean(conv, axis=-1, keepdims=True)
        xc = conv - mu
        var = jnp.mean(xc * xc, axis=-1, keepdims=True)
        y = xc * lax.rsqrt(var + LN_EPS) * lg_ref[...] + lb_ref[...]
        hc_ref[0, rows, :] = (y * _sigmoid(y)).astype(jnp.bfloat16)

    ukk = _dot(h, wkk_ref[...])
    lane = lax.broadcasted_iota(jnp.int32, (TM_IN, KAUG), 1)
    pos = pl.program_id(1) * TM_IN + lax.broadcasted_iota(jnp.int32, (TM_IN, KAUG), 0)
    feat = jnp.where(lane == HEAD_DIM, pos // POS_SPLIT,
                     jnp.where(lane == HEAD_DIM + 1, pos % POS_SPLIT, 0)).astype(jnp.float32)
    k_ref[0] = jnp.where(lane < HEAD_DIM, ukk, feat).astype(jnp.bfloat16)
    ki_ref[0] = ukk[:, HEAD_DIM:].astype(jnp.bfloat16)

    ut = ut_ref[...].reshape(ut_ref.shape[0] * ut_ref.shape[1], TM_IN)
    nq = N_HEADS * HEAD_DIM
    nqi = H_IDX * D_IDX
    qt_ref[0] = ut[:nq].astype(jnp.bfloat16)
    qit_ref[0] = ut[nq:nq + nqi].astype(jnp.bfloat16)
    vt = ut[nq + nqi:nq + nqi + HEAD_DIM].astype(jnp.bfloat16)
    ones_rows = (lax.broadcasted_iota(jnp.int32, (VAUG - HEAD_DIM, CK), 0) < 8).astype(jnp.bfloat16)
    for c in range(TM_IN // CK):
        vt_ref[0, c, :HEAD_DIM, :] = vt[:, c * CK:(c + 1) * CK]
        vt_ref[0, c, HEAD_DIM:, :] = ones_rows
    o = nq + nqi + HEAD_DIM
    wit_ref[0] = ut[o:o + H_IDX] * (H_IDX ** -0.5)


def _inproj(x, gn, wc, wg, bg, wkk, wt, conv_w, conv_b, ln_g, ln_b):
    B, L, D = x.shape
    nt = L // TM_IN
    const = lambda b, i: (0, 0)
    row = lambda b, i: (b, i, 0)
    col = lambda b, i: (b, 0, i)
    bf = jnp.bfloat16
    out_shape = (
        jax.ShapeDtypeStruct((B, L, C_CONV), bf),
        jax.ShapeDtypeStruct((N_STRIPS, B, L, 2 * D // N_STRIPS), bf),
        jax.ShapeDtypeStruct((B, L, KAUG), bf),
        jax.ShapeDtypeStruct((B, L, D_IDX), bf),
        jax.ShapeDtypeStruct((B, N_HEADS * HEAD_DIM, L), bf),
        jax.ShapeDtypeStruct((B, H_IDX * D_IDX, L), bf),
        jax.ShapeDtypeStruct((B, L // CK, VAUG, CK), bf),
        jax.ShapeDtypeStruct((B, H_IDX, L), jnp.float32),
    )
    out_specs = (
        pl.BlockSpec((1, TM_IN, C_CONV), row),
        pl.BlockSpec((N_STRIPS, 1, TM_IN, 2 * D // N_STRIPS), lambda b, i: (0, b, i, 0)),
        pl.BlockSpec((1, TM_IN, KAUG), row),
        pl.BlockSpec((1, TM_IN, D_IDX), row),
        pl.BlockSpec((1, N_HEADS * HEAD_DIM, TM_IN), col),
        pl.BlockSpec((1, H_IDX * D_IDX, TM_IN), col),
        pl.BlockSpec((1, TM_IN // CK, VAUG, CK), lambda b, i: (b, i, 0, 0)),
        pl.BlockSpec((1, H_IDX, TM_IN), col),
    )
    consts = (gn, wc, wg, bg, wkk, wt, conv_w, conv_b, ln_g, ln_b)
    in_specs = [pl.BlockSpec((1, TM_IN, D), row)] + [
        pl.BlockSpec(a.shape, lambda b, i, nd=a.ndim: (0,) * nd) for a in consts]
    return pl.pallas_call(
        _inproj_kernel,
        grid=(B, nt),
        in_specs=in_specs,
        out_specs=out_specs,
        out_shape=out_shape,
        scratch_shapes=[
            pltpu.VMEM((TM_IN, D), bf),
            pltpu.VMEM((N_STRIPS, CONV_PAD + TM_IN, LANES), jnp.float32),
            pltpu.VMEM((N_STRIPS, TM_IN, LANES), jnp.float32),
            pltpu.VMEM(wt.shape[:2] + (TM_IN,), jnp.float32),
        ],
        compiler_params=pltpu.CompilerParams(
            dimension_semantics=("parallel", "arbitrary"), vmem_limit_bytes=VMEM_LIMIT),
        name="inproj_conv",
    )(x, *consts)


def _dsa_kernel(qit_ref, wit_ref, ki_ref, qt_ref, k_ref, vt_ref, o_ref,
                score_ref, sb_ref, bias_ref, acc_ref, qa_ref, m_ref, a_ref, s_ref, p_ref,
                *, topk):
    j = pl.program_id(1)
    nck = j + 1
    t_q = j * TQ + lax.broadcasted_iota(jnp.int32, (CK, TQ), 1)
    row_iota = lax.broadcasted_iota(jnp.int32, (CK, TQ), 0)

    def score_chunk(c, carry):
        kic = ki_ref[0, pl.ds(pl.multiple_of(c * CK, CK), CK), :]
        score = jnp.zeros((CK, TQ), jnp.float32)
        for h in range(H_IDX):
            rel = _dot(kic, qit_ref[0, h * D_IDX:(h + 1) * D_IDX, :])
            score = score + wit_ref[0, h:h + 1, :] * jnp.maximum(rel, 0.0)
        causal = (c * CK + row_iota) <= t_q
        score = jnp.where(causal, score, -jnp.inf)
        score_ref[c] = score
        sb_ref[c] = score.astype(jnp.bfloat16)
        return carry

    lax.fori_loop(0, nck, score_chunk, 0)

    def sublane_allreduce(x, op):
        for sh in (4, 2, 1):
            x = op(x, pltpu.roll(x, sh, axis=0))
        return x

    def count(pred_fn):
        def body(c, acc):
            m = pred_fn(score_ref[c].reshape(CK // 8, 8, TQ)).astype(jnp.int32)
            return acc + jnp.sum(m.reshape(CK // 64, 8, 8, TQ), axis=0)
        acc = lax.fori_loop(0, nck, body, jnp.zeros((8, 8, TQ), jnp.int32))
        return sublane_allreduce(jnp.sum(acc, axis=0), jnp.add)

    def key_to_float(key):
        return pltpu.bitcast(jnp.where(key < 0, key ^ 0x7FFFFFFF, key), jnp.float32)

    def count16(cand):
        cand_b = jnp.concatenate([cand, cand], axis=0).astype(jnp.bfloat16)[None]
        one = jnp.ones((), jnp.bfloat16)
        zero = jnp.zeros((), jnp.bfloat16)

        def body(c, acc):
            sb = sb_ref[c].reshape(CK // 16, 16, TQ)
            parts = [jnp.where(sb[r:r + 1] >= cand_b, one, zero) for r in range(CK // 16)]
            while len(parts) > 1:
                parts = [parts[r] + parts[r + 1] for r in range(0, len(parts), 2)]
            return acc + parts[0][0].astype(jnp.float32)

        acc = lax.fori_loop(0, nck, body, jnp.zeros((16, TQ), jnp.float32))
        return sublane_allreduce(acc[:8] + acc[8:], jnp.add)

    def key16_to_float(k16):
        bits16 = jnp.where(k16 < 0, k16 ^ 0x7FFF, k16) & 0xFFFF
        return pltpu.bitcast(jnp.left_shift(bits16, 16), jnp.float32)

    def bit_step16(i, u):
        cand_u = u | jnp.left_shift(jnp.int32(1), 15 - i)
        cnt = count16(key16_to_float(cand_u - 32768))
        return jnp.where(cnt >= topk, cand_u, u)

    u16 = lax.fori_loop(0, 16, bit_step16, jnp.zeros((8, TQ), jnp.int32))
    few = u16 == 0
    t16_bits = pltpu.bitcast(key16_to_float(u16 - 32768), jnp.int32)
    base = jnp.where(t16_bits < 0, t16_bits ^ 0x7FFFFFFF, t16_bits) - 65536

    def bit_step(i, u):
        cand_u = u | jnp.left_shift(jnp.int32(1), 16 - i)
        cand = key_to_float(base + cand_u)
        cnt = count(lambda sc: sc >= cand[None])
        return jnp.where(cnt >= topk, cand_u, u)

    u = lax.fori_loop(0, 17, bit_step, jnp.zeros((8, TQ), jnp.int32))
    thr = jnp.where(few, -jnp.inf, key_to_float(base + u))
    cnt_ge = count(lambda sc: sc >= thr[None])
    has_tie = jnp.logical_and(cnt_ge > topk, jnp.logical_not(few))
    any_tie = jnp.max(has_tie.astype(jnp.int32)) > 0

    def select_plain():
        thr_fin = jnp.where(few, jnp.finfo(jnp.float32).min, thr)[None]

        def body(c, carry):
            sc = score_ref[c].reshape(CK // 8, 8, TQ)
            bias_ref[c] = jnp.where(sc >= thr_fin, 0.0, NEG_BIG).reshape(CK, TQ)
            return carry

        lax.fori_loop(0, nck, body, 0)

    def select_ties(n):
        def fn():
            cnt_gt = count(lambda sc: sc > thr[None])
            need = jnp.where(few, 0, topk - cnt_gt).astype(jnp.float32)[None]
            tri = (lax.broadcasted_iota(jnp.int32, (CK, CK), 1)
                   <= lax.broadcasted_iota(jnp.int32, (CK, CK), 0)).astype(jnp.bfloat16)
            before = jnp.zeros((1, TQ), jnp.float32)
            for c in range(n):
                sc = score_ref[c].reshape(CK // 8, 8, TQ)
                eq = sc == thr[None]
                eq01 = jnp.where(eq, 1.0, 0.0).reshape(CK, TQ).astype(jnp.bfloat16)
                local = _dot(tri, eq01)
                rank = (local + before).reshape(CK // 8, 8, TQ)
                eq_rank = jnp.where(eq, rank, 3e38)
                bias = jnp.where(sc > thr[None], 0.0,
                                 jnp.where(eq_rank <= need, 0.0, NEG_BIG))
                bias_ref[c] = bias.reshape(CK, TQ)
                before = before + local[CK - 1:CK, :]
        return fn

    def select_with_ties():
        lax.switch(j, [select_ties(n + 1) for n in range(score_ref.shape[0])])

    lax.cond(any_tie, select_with_ties, select_plain)

    frow = lax.broadcasted_iota(jnp.int32, (KAUG - HEAD_DIM, TQ), 0)
    for h in range(N_HEADS):
        slope = _alibi_slope(h)
        feat = jnp.where(frow == 0, slope * POS_SPLIT, jnp.where(frow == 1, slope, 0.0))
        qa_ref[h, :HEAD_DIM, :] = qt_ref[0, h * HEAD_DIM:(h + 1) * HEAD_DIM, :]
        qa_ref[h, HEAD_DIM:, :] = feat.astype(jnp.bfloat16)
    m_ref[...] = jnp.full(m_ref.shape, NEG_BIG, jnp.float32)
    acc_ref[...] = jnp.zeros(acc_ref.shape, jnp.float32)

    def att_chunk(c, carry):
        kc = k_ref[0, pl.ds(pl.multiple_of(c * CK, CK), CK), :]
        bias = bias_ref[c]
        vtc = vt_ref[0, c]
        for h in range(N_HEADS):
            s = _dot(kc, qa_ref[h]) + bias
            s_ref[h] = s
            mx = jnp.max(s.reshape(CK // 8, 8, TQ), axis=0)
            m_old = m_ref[h]
            m_new = jnp.maximum(m_old, sublane_allreduce(mx, jnp.maximum))
            a_ref[h] = jnp.exp(m_old - m_new)
            m_ref[h] = m_new
        for h in range(N_HEADS):
            p = jnp.exp(s_ref[h].reshape(CK // 8, 8, TQ) - m_ref[h][None])
            p_ref[h] = p.reshape(CK, TQ).astype(jnp.bfloat16)
        for h in range(N_HEADS):
            pv = _dot(vtc, p_ref[h])
            acc = acc_ref[h].reshape(VAUG // 8, 8, TQ)
            acc_ref[h] = (a_ref[h][None] * acc).reshape(VAUG, TQ) + pv
        return carry

    lax.fori_loop(0, nck, att_chunk, 0)

    hps = CK // HEAD_DIM
    for h in range(N_HEADS):
        acc = acc_ref[h, :HEAD_DIM, :].reshape(HEAD_DIM // 8, 8, TQ)
        denom = acc_ref[h, HEAD_DIM:HEAD_DIM + 8, :]
        r0 = (h % hps) * HEAD_DIM
        s_ref[h // hps, r0:r0 + HEAD_DIM, :] = (acc / denom[None]).reshape(HEAD_DIM, TQ)
    for g in range(N_HEADS // hps):
        o_ref[0, :, g * CK:(g + 1) * CK] = s_ref[g].T.astype(jnp.bfloat16)


def _dsa(qit, wit, ki, qt, k, vt):
    B, _, L = qt.shape
    nq = L // TQ
    nck = L // CK
    topk = min(TOPK_MAX, L // 4)
    per_b = lambda b, j: (b, 0, 0)
    col = lambda b, j: (b, 0, j)
    return pl.pallas_call(
        functools.partial(_dsa_kernel, topk=topk),
        grid=(B, nq),
        in_specs=[
            pl.BlockSpec((1, H_IDX * D_IDX, TQ), col),
            pl.BlockSpec((1, H_IDX, TQ), col),
            pl.BlockSpec((1, L, D_IDX), per_b),
            pl.BlockSpec((1, N_HEADS * HEAD_DIM, TQ), col),
            pl.BlockSpec((1, L, KAUG), per_b),
            pl.BlockSpec((1, nck, VAUG, CK), lambda b, j: (b, 0, 0, 0)),
        ],
        out_specs=pl.BlockSpec((1, TQ, N_HEADS * HEAD_DIM), lambda b, j: (b, j, 0)),
        out_shape=jax.ShapeDtypeStruct((B, L, N_HEADS * HEAD_DIM), jnp.bfloat16),
        scratch_shapes=[
            pltpu.VMEM((nck, CK, TQ), jnp.float32),
            pltpu.VMEM((nck, CK, TQ), jnp.bfloat16),
            pltpu.VMEM((nck, CK, TQ), jnp.float32),
            pltpu.VMEM((N_HEADS, VAUG, TQ), jnp.float32),
            pltpu.VMEM((N_HEADS, KAUG, TQ), jnp.bfloat16),
            pltpu.VMEM((N_HEADS, 8, TQ), jnp.float32),
            pltpu.VMEM((N_HEADS, 8, TQ), jnp.float32),
            pltpu.VMEM((N_HEADS, CK, TQ), jnp.float32),
            pltpu.VMEM((N_HEADS, CK, TQ), jnp.bfloat16),
        ],
        compiler_params=pltpu.CompilerParams(
            dimension_semantics=("parallel", "parallel"), vmem_limit_bytes=VMEM_LIMIT),
        name="dsa_attention",
    )(qit, wit, ki, qt, k, vt)


def _post_kernel(x_ref, hc_ref, at_ref, g_ref, wco_ref, wao_ref, wo_ref, nf_ref,
                 wfg_ref, wfu_ref, wfd_ref, nl_ref, o_ref, *, final_norm, ff_chunks):
    D = x_ref.shape[-1]
    y_conv = _dot(hc_ref[...], wco_ref[...])
    y_attn = _dot(at_ref[...], wao_ref[...])
    nb = g_ref.shape[0] // 2
    g_conv = jnp.concatenate([g_ref[c] for c in range(nb)], axis=1).astype(jnp.float32)
    g_attn = jnp.concatenate([g_ref[nb + c] for c in range(nb)], axis=1).astype(jnp.float32)
    merged = g_conv * y_conv + g_attn * y_attn
    x1 = x_ref[...] + _dot(merged.astype(jnp.bfloat16), wo_ref[...])

    ms = jnp.mean(x1 * x1, axis=-1, keepdims=True)
    hf = (x1 * lax.rsqrt(ms + RMS_EPS) * nf_ref[...]).astype(jnp.bfloat16)
    dff = wfg_ref.shape[1]
    fc = dff // ff_chunks
    x2 = x1
    for c in range(ff_chunks):
        gt = _dot(hf, wfg_ref[:, c * fc:(c + 1) * fc])
        up = _dot(hf, wfu_ref[:, c * fc:(c + 1) * fc])
        act = (gt * _sigmoid(gt) * up).astype(jnp.bfloat16)
        x2 = x2 + _dot(act, wfd_ref[c * fc:(c + 1) * fc, :])
    if final_norm:
        ms2 = jnp.mean(x2 * x2, axis=-1, keepdims=True)
        x2 = x2 * lax.rsqrt(ms2 + RMS_EPS) * nl_ref[...]
    o_ref[...] = x2


def _post(x2d, hc, at, g, wco, wao, wo, nf, wfg, wfu, wfd, nl, final_norm):
    T, D = x2d.shape
    const = lambda i: (0, 0)
    row = lambda i: (i, 0)
    single = pl.Buffered(1)

    def wspec(a):
        return pl.BlockSpec(a.shape, const, pipeline_mode=single)

    return pl.pallas_call(
        functools.partial(_post_kernel, final_norm=final_norm, ff_chunks=2),
        grid=(T // TM_POST,),
        in_specs=[
            pl.BlockSpec((TM_POST, D), row),
            pl.BlockSpec((TM_POST, hc.shape[1]), row),
            pl.BlockSpec((TM_POST, at.shape[1]), row),
            pl.BlockSpec((g.shape[0], TM_POST, g.shape[2]), lambda i: (0, i, 0)),
            wspec(wco), wspec(wao), wspec(wo), wspec(nf),
            wspec(wfg), wspec(wfu), wspec(wfd), wspec(nl),
        ],
        out_specs=pl.BlockSpec((TM_POST, D), row),
        out_shape=jax.ShapeDtypeStruct((T, D), jnp.float32),
        compiler_params=pltpu.CompilerParams(
            dimension_semantics=("parallel",), vmem_limit_bytes=VMEM_LIMIT),
        name="post",
    )(x2d, hc, at, g, wco, wao, wo, nf, wfg, wfu, wfd, nl)


def kernel(x, norm_mix, w_in, b_gate, conv_w, conv_b, conv_ln_g, conv_ln_b, w_conv_out,
           w_attn_out, w_out, norm_ffn, w_ffn_gate, w_ffn_up, w_ffn_down, norm_final):
    B, L, D = x.shape
    depth = w_in.shape[0]
    bf = jnp.bfloat16
    nq = N_HEADS * HEAD_DIM
    nqi = H_IDX * D_IDX
    o_q = 2 * C_CONV
    o_k = o_q + nq
    o_v = o_k + HEAD_DIM
    o_qi = o_v + HEAD_DIM
    o_ki = o_qi + nqi
    o_wi = o_ki + D_IDX
    o_g = o_wi + H_IDX

    for layer in range(depth):
        w = w_in[layer]
        wc = w[:, :o_q].astype(bf)
        gcols = (w.shape[1] - o_g) // N_STRIPS
        wg = w[:, o_g:].reshape(D, N_STRIPS, gcols).transpose(1, 0, 2).astype(bf)
        wkk = jnp.concatenate([w[:, o_k:o_v], w[:, o_ki:o_wi]], axis=1).astype(bf)
        wt = jnp.concatenate([
            w[:, o_q:o_k] * (HEAD_DIM ** -0.5),
            w[:, o_qi:o_ki] * (D_IDX ** -0.5),
            w[:, o_v:o_qi],
            w[:, o_wi:o_g],
            jnp.zeros((D, 128 - HEAD_DIM - H_IDX), w.dtype),
        ], axis=1).T.astype(bf)
        wt = wt.reshape(N_STRIPS, wt.shape[0] // N_STRIPS, D)

        hc, gate, k, ki, qt, qit, vt, wit = _inproj(
            x, norm_mix[layer][None, :], wc, wg, b_gate[layer].reshape(N_STRIPS, 1, gcols), wkk, wt,
            conv_w[layer].reshape(CONV_WIDTH, N_STRIPS, LANES).transpose(1, 0, 2),
            conv_b[layer].reshape(N_STRIPS, 1, LANES), conv_ln_g[layer][None, :],
            conv_ln_b[layer][None, :])
        at = _dsa(qit, wit, ki, qt, k, vt)
        x = _post(
            x.reshape(B * L, D), hc.reshape(B * L, C_CONV), at.reshape(B * L, nq),
            gate.reshape(N_STRIPS, B * L, gcols),
            w_conv_out[layer].astype(bf), w_attn_out[layer].astype(bf), w_out[layer].astype(bf),
            norm_ffn[layer][None, :], w_ffn_gate[layer].astype(bf), w_ffn_up[layer].astype(bf),
            w_ffn_down[layer].astype(bf), norm_final[None, :],
            final_norm=(layer == depth - 1)).reshape(B, L, D)
    return x
```
